```python
import math
import jax, jax.numpy as jnp
from jax import lax
import numpy as np

D_MODEL = 2048
BATCH = 1
SEQ = 8192
DEPTH = 1

HEAD_DIM = 128
A_HEADS = 8
A_KV_HEADS = 2
B_HEADS = 8
B_KV_HEADS = 2
WINDOW = 128
BLOCK = 128
N_BUCKETS = 32
MAX_DISTANCE = 128
GRID_W = 64
ROPE_THETA = 10000.0
N_EXPERTS = 64
TOP_K = 8
N_GROUPS = 8
TOPK_GROUPS = 4
EXPERT_HIDDEN = 512
SHARED_HIDDEN = 512
ROUTED_SCALE = 2.5
MOE_BLOCK = 128
LN_EPS = 1e-5
RMS_EPS = 1e-6
DEEPNORM_ALPHA = (2 * DEPTH) ** 0.25
DEEPNORM_BETA = (8 * DEPTH) ** -0.25

A_Q = A_HEADS * HEAD_DIM
A_KV = A_KV_HEADS * HEAD_DIM
B_Q = B_HEADS * HEAD_DIM
B_KV = B_KV_HEADS * HEAD_DIM
IN_COLS = A_Q + 2 * A_KV + B_Q + 2 * B_KV + 2 * D_MODEL

kernel_name = "hybrid_window_axial_moe_encoder"


def layer_norm(x, g, b):
    xf = x.astype(jnp.float32)
    mu = jnp.mean(xf, -1, keepdims=True)
    var = jnp.mean(jnp.square(xf - mu), -1, keepdims=True)
    return ((xf - mu) * lax.rsqrt(var + LN_EPS) * g.astype(jnp.float32) + b.astype(jnp.float32)).astype(x.dtype)


def rms_norm(x, g):
    xf = x.astype(jnp.float32)
    y = xf * lax.rsqrt(jnp.mean(jnp.square(xf), -1, keepdims=True) + RMS_EPS)
    return (y * g.astype(jnp.float32)).astype(x.dtype)


def t5_bucket(rel):
    nb = N_BUCKETS // 2
    ret = jnp.where(rel > 0, nb, 0)
    n = jnp.abs(rel)
    max_exact = nb // 2
    nf = jnp.maximum(n, 1).astype(jnp.float32)
    large = max_exact + (jnp.log(nf / max_exact) / math.log(MAX_DISTANCE / max_exact) * (nb - max_exact)).astype(jnp.int32)
    large = jnp.minimum(large, nb - 1)
    return ret + jnp.where(n < max_exact, n, large)


def windowed_sink_attention(q, k, v, sink, rel_table):
    Bsz, S = q.shape[0], q.shape[1]
    nblk = S // BLOCK
    G = A_HEADS // A_KV_HEADS
    qb = q.reshape(Bsz, nblk, BLOCK, A_KV_HEADS, G, HEAD_DIM)
    pad = ((0, 0), (BLOCK, BLOCK), (0, 0))

    def band(t):
        tp = jnp.pad(t, pad)
        return jnp.concatenate(
            [tp[:, j * BLOCK:(j + nblk) * BLOCK].reshape(Bsz, nblk, BLOCK, A_KV_HEADS, HEAD_DIM) for j in range(3)],
            axis=2)

    kb, vb = band(k), band(v)
    s = jnp.einsum('bnqkgd,bnckd->bnkgqc', qb, kb, preferred_element_type=jnp.float32) * (HEAD_DIM ** -0.5)
    qi = jnp.arange(BLOCK)[:, None]
    c = jnp.arange(3 * BLOCK)[None, :]
    rel = c - BLOCK - qi
    bias = rel_table.astype(jnp.float32)[t5_bucket(rel)]
    bias = jnp.moveaxis(bias, -1, 0).reshape(A_KV_HEADS, G, BLOCK, 3 * BLOCK)
    kpos = (jnp.arange(nblk)[:, None] - 1) * BLOCK + jnp.arange(3 * BLOCK)[None, :]
    valid = (jnp.abs(rel) <= WINDOW)[None] & ((kpos >= 0) & (kpos < S))[:, None, :]
    s = jnp.where(valid[None, :, None, None], s + bias, -jnp.inf)
    sk = sink.astype(jnp.float32).reshape(A_KV_HEADS, G)[:, :, None, None]
    m = jnp.maximum(jnp.max(s, -1, keepdims=True), sk)
    p = jnp.exp(s - m)
    p = p / (jnp.sum(p, -1, keepdims=True) + jnp.exp(sk - m))
    o = jnp.einsum('bnkgqc,bnckd->bnqkgd', p.astype(v.dtype), vb)
    return o.reshape(Bsz, S, A_Q)


def axial_rope(x, cos_r, sin_r, cos_c, sin_c):
    half = HEAD_DIM // 2
    qd = half // 2

    def rot(xs, cs, sn):
        x1, x2 = xs[..., :qd], xs[..., qd:]
        return jnp.concatenate([x1 * cs - x2 * sn, x2 * cs + x1 * sn], -1)

    return jnp.concatenate([rot(x[..., :half], cos_r, sin_r), rot(x[..., half:], cos_c, sin_c)], -1)


def axial_dense_attention(q, k, v, qn_g, kn_g):
    Bsz, S = q.shape[0], q.shape[1]
    ROWS = S // GRID_W
    nblk = S // BLOCK
    G = B_HEADS // B_KV_HEADS
    row = jnp.broadcast_to(jnp.arange(ROWS)[:, None], (ROWS, GRID_W)).reshape(S).astype(jnp.float32)
    col = jnp.broadcast_to(jnp.arange(GRID_W)[None, :], (ROWS, GRID_W)).reshape(S).astype(jnp.float32)
    half = HEAD_DIM // 2
    inv = ROPE_THETA ** (-jnp.arange(0, half, 2, dtype=jnp.float32) / half)
    ang_r = (row[:, None] * inv)[:, None, :]
    ang_c = (col[:, None] * inv)[:, None, :]
    dt = q.dtype
    tabs = (jnp.cos(ang_r).astype(dt), jnp.sin(ang_r).astype(dt), jnp.cos(ang_c).astype(dt), jnp.sin(ang_c).astype(dt))
    qh = axial_rope(rms_norm(q.reshape(Bsz, S, B_HEADS, HEAD_DIM), qn_g), *tabs)
    kh = axial_rope(rms_norm(k.reshape(Bsz, S, B_KV_HEADS, HEAD_DIM), kn_g), *tabs)
    vh = v.reshape(Bsz, S, B_KV_HEADS, HEAD_DIM)
    qb = qh.reshape(Bsz, nblk, BLOCK, B_KV_HEADS, G, HEAD_DIM).transpose(1, 0, 2, 3, 4, 5)
    scale = HEAD_DIM ** -0.5

    def one_block(qblk):
        s = jnp.einsum('bqkgd,bskd->bkgqs', qblk, kh, preferred_element_type=jnp.float32) * scale
        p = jax.nn.softmax(s, axis=-1)
        return jnp.einsum('bkgqs,bskd->bqkgd', p.astype(vh.dtype), vh)

    o = lax.map(one_block, qb)
    return o.transpose(1, 0, 2, 3, 4, 5).reshape(Bsz, S, B_Q)


def token_mixer(h, w_in, b_gate, sink, rel_table, qn_g, kn_g, w_branch_a, w_branch_b, w_out):
    proj = jnp.einsum('bsd,dc->bsc', h, w_in)
    bounds = [int(i) for i in np.cumsum([A_Q, A_KV, A_KV, B_Q, B_KV, B_KV, D_MODEL])]
    qa, ka, va, qb, kb, vb, ga, gb = jnp.split(proj, bounds, axis=-1)
    oa = windowed_sink_attention(qa, ka, va, sink, rel_table)
    ob = axial_dense_attention(qb, kb, vb, qn_g, kn_g)
    gate_a = jax.nn.sigmoid(ga + b_gate[0])
    gate_b = jax.nn.sigmoid(gb + b_gate[1])
    merged = gate_a * jnp.einsum('bsc,cd->bsd', oa, w_branch_a) + gate_b * jnp.einsum('bsc,cd->bsd', ob, w_branch_b)
    return jnp.einsum('bsd,de->bse', merged, w_out)


def moe_ffn(h, w_router, e_bias, w_gate, w_up, w_down, ws_gate, ws_up, ws_down):
    Bsz, S, D = h.shape
    N = Bsz * S
    xt = h.reshape(N, D)
    scores = jax.nn.sigmoid(jnp.einsum('nd,de->ne', xt, w_router, preferred_element_type=jnp.float32))
    biased = scores + e_bias.astype(jnp.float32)
    group_score = lax.top_k(biased.reshape(N, N_GROUPS, N_EXPERTS // N_GROUPS), 2)[0].sum(-1)
    _, gidx = lax.top_k(group_score, TOPK_GROUPS)
    gmask = jnp.sum(jax.nn.one_hot(gidx, N_GROUPS, dtype=jnp.float32), axis=1) > 0
    masked = jnp.where(jnp.repeat(gmask, N_EXPERTS // N_GROUPS, axis=1), biased, -jnp.inf)
    _, eidx = lax.top_k(masked, TOP_K)
    wsel = jnp.take_along_axis(scores, eidx, axis=1)
    wsel = wsel / jnp.sum(wsel, -1, keepdims=True) * ROUTED_SCALE

    A = N * TOP_K
    e_flat = eidx.reshape(A)
    tok_flat = jnp.arange(A, dtype=jnp.int32) // TOP_K
    w_flat = wsel.reshape(A)
    order = jnp.argsort(e_flat)
    e_sorted = e_flat[order]
    counts = jnp.bincount(e_flat, length=N_EXPERTS)
    starts = jnp.cumsum(counts) - counts
    padded = (counts + MOE_BLOCK - 1) // MOE_BLOCK * MOE_BLOCK
    pends = jnp.cumsum(padded)
    pstarts = pends - padded
    dest = pstarts[e_sorted] + jnp.arange(A, dtype=jnp.int32) - starts[e_sorted]
    nb = -(-A // MOE_BLOCK) + N_EXPERTS
    P = nb * MOE_BLOCK
    tok_buf = jnp.full((P,), N, jnp.int32).at[dest].set(tok_flat[order])
    w_buf = jnp.zeros((P,), jnp.float32).at[dest].set(w_flat[order])
    block_e = jnp.minimum(jnp.searchsorted(pends, jnp.arange(nb) * MOE_BLOCK, side='right'), N_EXPERTS - 1)
    x_pad = jnp.concatenate([xt, jnp.zeros((1, D), xt.dtype)], axis=0)

    def run_block(args):
        e, toks, wts = args
        xb = x_pad[toks]
        hid = jax.nn.silu(xb @ w_gate[e]) * (xb @ w_up[e])
        return (hid @ w_down[e]) * wts[:, None].astype(xb.dtype)

    yb = lax.map(run_block, (block_e, tok_buf.reshape(nb, MOE_BLOCK), w_buf.reshape(nb, MOE_BLOCK)))
    routed = jnp.zeros((N + 1, D), jnp.float32).at[tok_buf].add(yb.reshape(P, D).astype(jnp.float32))[:N]
    shared = (jax.nn.silu(xt @ ws_gate) * (xt @ ws_up)) @ ws_down
    return (routed.astype(h.dtype) + shared).reshape(Bsz, S, D)


def setup_inputs(seed: int = 0) -> dict:
    key = jax.random.key(seed)
    ks = jax.random.split(key, 24)
    f32 = jnp.float32
    L, D, E, F, Fs = DEPTH, D_MODEL, N_EXPERTS, EXPERT_HIDDEN, SHARED_HIDDEN

    def nrm(k, shape, scale):
        return jax.random.normal(k, shape, f32) * scale

    colscale = np.concatenate([np.ones(A_Q), np.ones(A_KV), np.full(A_KV, DEEPNORM_BETA), np.ones(B_Q), np.ones(B_KV),
                               np.full(B_KV, DEEPNORM_BETA), np.ones(2 * D)]).astype(np.float32)
    return {
        "x": nrm(ks[0], (BATCH, SEQ, D), 1.0),
        "w_in": nrm(ks[1], (L, D, IN_COLS), D ** -0.5) * jnp.asarray(colscale),
        "b_gate": nrm(ks[2], (L, 2, D), 0.02),
        "attn_sink": nrm(ks[3], (L, A_HEADS), 0.5),
        "rel_bias_table": nrm(ks[4], (N_BUCKETS, A_HEADS), 0.5),
        "q_norm_g": 1.0 + nrm(ks[5], (L, HEAD_DIM), 0.02),
        "k_norm_g": 1.0 + nrm(ks[6], (L, HEAD_DIM), 0.02),
        "w_branch_a": nrm(ks[7], (L, A_Q, D), A_Q ** -0.5),
        "w_branch_b": nrm(ks[8], (L, B_Q, D), B_Q ** -0.5),
        "w_out": nrm(ks[9], (L, D, D), D ** -0.5 * DEEPNORM_BETA),
        "ln1_g": 1.0 + nrm(ks[10], (L, D), 0.02),
        "ln1_b": nrm(ks[11], (L, D), 0.02),
        "w_router": nrm(ks[12], (L, D, E), D ** -0.5),
        "router_bias": nrm(ks[13], (L, E), 0.01),
        "w_exp_gate": nrm(ks[14], (L, E, D, F), D ** -0.5),
        "w_exp_up": nrm(ks[15], (L, E, D, F), D ** -0.5),
        "w_exp_down": nrm(ks[16], (L, E, F, D), F ** -0.5 * DEEPNORM_BETA),
        "w_sh_gate": nrm(ks[17], (L, D, Fs), D ** -0.5),
        "w_sh_up": nrm(ks[18], (L, D, Fs), D ** -0.5),
        "w_sh_down": nrm(ks[19], (L, Fs, D), Fs ** -0.5 * DEEPNORM_BETA),
        "ln2_g": 1.0 + nrm(ks[20], (L, D), 0.02),
        "ln2_b": nrm(ks[21], (L, D), 0.02),
    }


def reference(x, w_in, b_gate, attn_sink, rel_bias_table, q_norm_g, k_norm_g, w_branch_a, w_branch_b, w_out,
              ln1_g, ln1_b, w_router, router_bias, w_exp_gate, w_exp_up, w_exp_down, w_sh_gate, w_sh_up, w_sh_down,
              ln2_g, ln2_b):
    h = x
    for l in range(DEPTH):
        mix = token_mixer(h, w_in[l], b_gate[l], attn_sink[l], rel_bias_table, q_norm_g[l], k_norm_g[l],
                          w_branch_a[l], w_branch_b[l], w_out[l])
        h = layer_norm(DEEPNORM_ALPHA * h + mix, ln1_g[l], ln1_b[l])
        ffn = moe_ffn(h, w_router[l], router_bias[l], w_exp_gate[l], w_exp_up[l], w_exp_down[l],
                      w_sh_gate[l], w_sh_up[l], w_sh_down[l])
        h = layer_norm(DEEPNORM_ALPHA * h + ffn, ln2_g[l], ln2_b[l])
    return h
```

```python
import functools
import math

import jax
import jax.numpy as jnp
from jax import lax
from jax.experimental import pallas as pl
from jax.experimental.pallas import tpu as pltpu

F32 = jnp.float32
BF16 = jnp.bfloat16
I32 = jnp.int32

HEAD_DIM = 128
A_HEADS = 8
A_KV_HEADS = 2
B_HEADS = 8
B_KV_HEADS = 2
WINDOW = 128
BLOCK = 128
N_BUCKETS = 32
MAX_DISTANCE = 128
GRID_W = 64
ROPE_THETA = 10000.0
N_GROUPS = 8
TOPK_GROUPS = 4
TOP_K = 8
ROUTED_SCALE = 2.5
LN_EPS = 1e-5
RMS_EPS = 1e-6
NEG_BIG = -1e30

V7X_VMEM_LIMIT = 56 * 1024 * 1024
EXPERT_ROWS = 256
ROUTER_TILE = 256
MOVE_TILE = 128


def _params(n_axes):
    return pltpu.CompilerParams(dimension_semantics=("arbitrary",) * n_axes,
                                vmem_limit_bytes=V7X_VMEM_LIMIT)


def _mm_kernel(x_ref, w_ref, o_ref):
    o_ref[...] = jnp.dot(x_ref[...], w_ref[...], preferred_element_type=F32).astype(o_ref.dtype)


def _matmul(x, w, tm, tn, out_dtype, name):
    m, k = x.shape
    n = w.shape[1]
    return pl.pallas_call(
        _mm_kernel,
        grid=(m // tm, n // tn),
        in_specs=[pl.BlockSpec((tm, k), lambda i, j: (i, 0)),
                  pl.BlockSpec((k, tn), lambda i, j: (0, j))],
        out_specs=pl.BlockSpec((tm, tn), lambda i, j: (i, j)),
        out_shape=jax.ShapeDtypeStruct((m, n), out_dtype),
        compiler_params=_params(2),
        name=name,
    )(x, w)


def _qkprep_kernel(x_ref, g_ref, cos_ref, sa_ref, sb_ref, o_ref):
    x = x_ref[...].astype(F32)
    ms = jnp.mean(x * x, axis=-1, keepdims=True)
    y = x * lax.rsqrt(ms + RMS_EPS) * g_ref[...]
    out = y * cos_ref[...] + pltpu.roll(y, 96, 1) * sa_ref[...] + pltpu.roll(y, 32, 1) * sb_ref[...]
    o_ref[...] = out.astype(o_ref.dtype)


def _qkprep(qkv, gains, cos_t, sa_t, sb_t, col0, n_heads, tr):
    s = qkv.shape[0]
    c0 = col0 // HEAD_DIM
    return pl.pallas_call(
        _qkprep_kernel,
        grid=(s // tr, n_heads),
        in_specs=[pl.BlockSpec((tr, HEAD_DIM), lambda i, h: (i, c0 + h)),
                  pl.BlockSpec((None, 1, HEAD_DIM), lambda i, h: (h, 0, 0)),
                  pl.BlockSpec((tr, HEAD_DIM), lambda i, h: (i, 0)),
                  pl.BlockSpec((tr, HEAD_DIM), lambda i, h: (i, 0)),
                  pl.BlockSpec((tr, HEAD_DIM), lambda i, h: (i, 0))],
        out_specs=pl.BlockSpec((tr, HEAD_DIM), lambda i, h: (i, h)),
        out_shape=jax.ShapeDtypeStruct((s, n_heads * HEAD_DIM), BF16),
        compiler_params=_params(2),
        name="qk_prep",
    )(qkv, gains, cos_t, sa_t, sb_t)


def _window_kernel(tab_ref, sink_ref, bucket_ref, q_ref, kp_ref, kc_ref, kn_ref, vp_ref, vc_ref, vn_ref,
                   o_ref, bias_scr):
    kvh = pl.program_id(0)
    n = pl.program_id(1)
    nb = pl.num_programs(1)
    g_per = A_HEADS // A_KV_HEADS

    @pl.when((kvh == 0) & (n == 0))
    def _():
        bucket = bucket_ref[...]
        for h in range(A_HEADS):
            acc = jnp.full(bucket.shape, NEG_BIG, F32)
            for b in range(N_BUCKETS):
                acc = jnp.where(bucket == b, tab_ref[b * A_HEADS + h], acc)
            bias_scr[h] = acc

    q = q_ref[...]
    q4 = jnp.concatenate([q[:, g * HEAD_DIM:(g + 1) * HEAD_DIM] for g in range(g_per)], axis=0)
    kcat = jnp.concatenate([kp_ref[...], kc_ref[...], kn_ref[...]], axis=0)
    vcat = jnp.concatenate([vp_ref[...], vc_ref[...], vn_ref[...]], axis=0)
    s = lax.dot_general(q4, kcat, (((1,), (1,)), ((), ())), preferred_element_type=F32)
    s = s * (HEAD_DIM ** -0.5)
    bias4 = bias_scr[pl.ds(kvh * g_per, g_per)]
    s = s + bias4.reshape(g_per * BLOCK, 3 * BLOCK)
    col = lax.broadcasted_iota(I32, s.shape, 1)
    lo = jnp.where(n == 0, BLOCK, 0)
    hi = jnp.where(n == nb - 1, 2 * BLOCK, 3 * BLOCK)
    s = jnp.where((col >= lo) & (col < hi), s, NEG_BIG)
    sk = jnp.concatenate([jnp.full((BLOCK, 1), sink_ref[kvh * g_per + g], F32) for g in range(g_per)], axis=0)
    m = jnp.maximum(jnp.max(s, axis=-1, keepdims=True), sk)
    p = jnp.exp(s - m)
    den = jnp.sum(p, axis=-1, keepdims=True) + jnp.exp(sk - m)
    o = jnp.dot(p.astype(BF16), vcat, preferred_element_type=F32) / den
    o_ref[...] = jnp.concatenate([o[g * BLOCK:(g + 1) * BLOCK] for g in range(g_per)], axis=1).astype(o_ref.dtype)


def _window_attention(qkv, tab_flat, sink, bucket, q_col0, k_col0, v_col0):
    s = qkv.shape[0]
    nb = s // BLOCK
    g_per = A_HEADS // A_KV_HEADS
    qw = g_per * HEAD_DIM
    qc, kc, vc = q_col0 // qw, k_col0 // HEAD_DIM, v_col0 // HEAD_DIM

    def kv_spec(c0, shift):
        return pl.BlockSpec((BLOCK, HEAD_DIM),
                            lambda h, n: (jnp.clip(n + shift, 0, nb - 1), c0 + h))

    smem = pl.BlockSpec(memory_space=pltpu.SMEM)
    return pl.pallas_call(
        _window_kernel,
        grid=(A_KV_HEADS, nb),
        in_specs=[smem, smem,
                  pl.BlockSpec((BLOCK, 3 * BLOCK), lambda h, n: (0, 0)),
                  pl.BlockSpec((BLOCK, qw), lambda h, n: (n, qc + h)),
                  kv_spec(kc, -1), kv_spec(kc, 0), kv_spec(kc, 1),
                  kv_spec(vc, -1), kv_spec(vc, 0), kv_spec(vc, 1)],
        out_specs=pl.BlockSpec((BLOCK, qw), lambda h, n: (n, h)),
        out_shape=jax.ShapeDtypeStruct((s, A_HEADS * HEAD_DIM), BF16),
        scratch_shapes=[pltpu.VMEM((A_HEADS, BLOCK, 3 * BLOCK), F32)],
        compiler_params=_params(2),
        name="window_attn",
    )(tab_flat, sink, bucket, qkv, qkv, qkv, qkv, qkv, qkv, qkv)


def _dense_kernel(q_ref, k_ref, v_ref, o_ref, m_scr, l_scr, acc_scr, *, tk):
    tq = q_ref.shape[0]
    s_len = k_ref.shape[0]
    g_per = B_HEADS // B_KV_HEADS
    q = q_ref[...]
    q4 = jnp.concatenate([q[:, g * HEAD_DIM:(g + 1) * HEAD_DIM] for g in range(g_per)], axis=0)
    m_scr[...] = jnp.full(m_scr.shape, -jnp.inf, F32)
    l_scr[...] = jnp.zeros(l_scr.shape, F32)
    acc_scr[...] = jnp.zeros(acc_scr.shape, F32)

    def body(c, carry):
        off = pl.multiple_of(c * tk, tk)
        k = k_ref[pl.ds(off, tk), :]
        v = v_ref[pl.ds(off, tk), :]
        s = lax.dot_general(q4, k, (((1,), (1,)), ((), ())), preferred_element_type=F32)
        m_prev = m_scr[...]
        m_new = jnp.maximum(m_prev, jnp.max(s, axis=-1, keepdims=True))
        a = jnp.exp(m_prev - m_new)
        p = jnp.exp(s - m_new)
        l_scr[...] = a * l_scr[...] + jnp.sum(p, axis=-1, keepdims=True)
        acc_scr[...] = a * acc_scr[...] + jnp.dot(p.astype(BF16), v, preferred_element_type=F32)
        m_scr[...] = m_new
        return carry

    lax.fori_loop(0, s_len // tk, body, 0)
    o = acc_scr[...] / l_scr[...]
    o_ref[...] = jnp.concatenate([o[g * tq:(g + 1) * tq] for g in range(g_per)], axis=1).astype(o_ref.dtype)


def _dense_attention(qk, qkv, v_col0, tq, tk):
    s = qk.shape[0]
    g_per = B_HEADS // B_KV_HEADS
    qw = g_per * HEAD_DIM
    kc = (B_HEADS * HEAD_DIM) // HEAD_DIM
    vc = v_col0 // HEAD_DIM
    return pl.pallas_call(
        functools.partial(_dense_kernel, tk=tk),
        grid=(B_KV_HEADS, s // tq),
        in_specs=[pl.BlockSpec((tq, qw), lambda h, i: (i, h)),
                  pl.BlockSpec((s, HEAD_DIM), lambda h, i: (0, kc + h)),
                  pl.BlockSpec((s, HEAD_DIM), lambda h, i: (0, vc + h))],
        out_specs=pl.BlockSpec((tq, qw), lambda h, i: (i, h)),
        out_shape=jax.ShapeDtypeStruct((s, B_HEADS * HEAD_DIM), BF16),
        scratch_shapes=[pltpu.VMEM((g_per * tq, 1), F32),
                        pltpu.VMEM((g_per * tq, 1), F32),
                        pltpu.VMEM((g_per * tq, HEAD_DIM), F32)],
        compiler_params=_params(2),
        name="dense_attn",
    )(qk, qk, qkv)


def _layer_norm(y, g, b):
    mu = jnp.mean(y, axis=-1, keepdims=True)
    yc = y - mu
    var = jnp.mean(yc * yc, axis=-1, keepdims=True)
    return yc * lax.rsqrt(var + LN_EPS) * g + b


def _merge_kernel(oa_ref, ob_ref, ga_ref, gb_ref, bg_ref, x_ref, wa_ref, wb_ref, wo_ref, lg_ref, lb_ref,
                  h_ref, *, alpha):
    ma = jnp.dot(oa_ref[...], wa_ref[...], preferred_element_type=F32)
    mb = jnp.dot(ob_ref[...], wb_ref[...], preferred_element_type=F32)
    gate_a = jax.nn.sigmoid(ga_ref[...].astype(F32) + bg_ref[0:1, :])
    gate_b = jax.nn.sigmoid(gb_ref[...].astype(F32) + bg_ref[1:2, :])
    merged = gate_a * ma + gate_b * mb
    mix = jnp.dot(merged.astype(BF16), wo_ref[...], preferred_element_type=F32)
    h_ref[...] = _layer_norm(alpha * x_ref[...] + mix, lg_ref[...], lb_ref[...])


def _merge(oa, ob, gates, b_gate, x2, wa, wb, wo, ln_g, ln_b, alpha, tm):
    s, d = x2.shape
    ca = oa.shape[1]

    def whole(shape):
        return pl.BlockSpec(shape, lambda i: (0,) * len(shape), pipeline_mode=pl.Buffered(1))

    return pl.pallas_call(
        functools.partial(_merge_kernel, alpha=alpha),
        grid=(s // tm,),
        in_specs=[pl.BlockSpec((tm, ca), lambda i: (i, 0)),
                  pl.BlockSpec((tm, ca), lambda i: (i, 0)),
                  pl.BlockSpec((tm, d), lambda i: (i, 0)),
                  pl.BlockSpec((tm, d), lambda i: (i, 1)),
                  whole((2, d)),
                  pl.BlockSpec((tm, d), lambda i: (i, 0)),
                  whole((ca, d)), whole((ca, d)), whole((d, d)),
                  whole((1, d)), whole((1, d))],
        out_specs=pl.BlockSpec((tm, d), lambda i: (i, 0)),
        out_shape=jax.ShapeDtypeStruct((s, d), F32),
        compiler_params=_params(1),
        name="merge_ln1",
    )(oa, ob, gates, gates, b_gate, x2, wa, wb, wo, ln_g, ln_b)


def _split_bf16(a):
    hi = a.astype(BF16)
    lo = (a - hi.astype(F32)).astype(BF16)
    return hi, lo


def _router_kernel(h_ref, wr_ref, rb_ref, e_ref, w_ref, r_ref, cnt_ref, carry_scr):
    i = pl.program_id(0)
    tt = h_ref.shape[0]
    n_e = wr_ref.shape[0]
    per = n_e // N_GROUPS

    @pl.when(i == 0)
    def _():
        carry_scr[...] = jnp.zeros(carry_scr.shape, F32)

    h_hi, h_lo = _split_bf16(h_ref[...])
    w_hi, w_lo = _split_bf16(wr_ref[...])
    dn = (((1,), (1,)), ((), ()))
    logits = (lax.dot_general(w_hi, h_hi, dn, preferred_element_type=F32)
              + lax.dot_general(w_hi, h_lo, dn, preferred_element_type=F32)
              + lax.dot_general(w_lo, h_hi, dn, preferred_element_type=F32))
    scores = jax.nn.sigmoid(logits)
    biased = scores + rb_ref[...]

    sc3 = scores.reshape(N_GROUPS, per, tt)
    b3 = biased.reshape(N_GROUPS, per, tt)
    gi = lax.broadcasted_iota(I32, b3.shape, 0)
    ji = lax.broadcasted_iota(I32, b3.shape, 1)
    ei = gi * per + ji

    m1 = jnp.max(b3, axis=1, keepdims=True)
    first = jnp.min(jnp.where(b3 == m1, ji, per), axis=1, keepdims=True)
    m2 = jnp.max(jnp.where(ji == first, -jnp.inf, b3), axis=1, keepdims=True)
    gs = m1 + m2

    g1 = lax.broadcasted_iota(I32, gs.shape, 0)
    gsel = jnp.zeros(gs.shape, F32)
    cur = gs
    for _ in range(TOPK_GROUPS):
        mx = jnp.max(cur, axis=0, keepdims=True)
        fi = jnp.min(jnp.where(cur == mx, g1, N_GROUPS), axis=0, keepdims=True)
        pick = g1 == fi
        gsel = jnp.where(pick, 1.0, gsel)
        cur = jnp.where(pick, -jnp.inf, cur)

    masked = jnp.where(gsel > 0.5, b3, -jnp.inf)
    sel = jnp.zeros(b3.shape, F32)
    picks, e_rows, w_rows = [], [], []
    for _ in range(TOP_K):
        mx = jnp.max(jnp.max(masked, axis=0, keepdims=True), axis=1, keepdims=True)
        fi = jnp.min(jnp.min(jnp.where(masked == mx, ei, n_e), axis=0, keepdims=True), axis=1, keepdims=True)
        pick = ei == fi
        picks.append(pick)
        e_rows.append(fi.reshape(1, tt))
        w_rows.append(jnp.sum(jnp.sum(jnp.where(pick, sc3, 0.0), axis=0, keepdims=True), axis=1,
                              keepdims=True).reshape(1, tt))
        sel = jnp.where(pick, 1.0, sel)
        masked = jnp.where(pick, -jnp.inf, masked)

    wsum = w_rows[0]
    for k in range(1, TOP_K):
        wsum = wsum + w_rows[k]

    sel2 = sel.reshape(n_e, tt)
    rr = lax.broadcasted_iota(I32, (tt, tt), 0)
    cc = lax.broadcasted_iota(I32, (tt, tt), 1)
    upper = jnp.where(rr < cc, 1.0, 0.0).astype(BF16)
    before = jnp.dot(sel2.astype(BF16), upper, preferred_element_type=F32) + carry_scr[...]
    before3 = before.reshape(N_GROUPS, per, tt)
    for k in range(TOP_K):
        rk = jnp.sum(jnp.sum(jnp.where(picks[k], before3, 0.0), axis=0, keepdims=True), axis=1, keepdims=True)
        e_ref[k:k + 1, :] = e_rows[k]
        w_ref[k:k + 1, :] = w_rows[k] / wsum * ROUTED_SCALE
        r_ref[k:k + 1, :] = rk.reshape(1, tt).astype(I32)

    carry_scr[...] = carry_scr[...] + jnp.sum(sel2, axis=1, keepdims=True)
    cnt_ref[...] = carry_scr[...].astype(I32)


def _router(h1, wr_t, rb_col, tt):
    s, d = h1.shape
    n_e = wr_t.shape[0]
    out3 = lambda dt: jax.ShapeDtypeStruct((TOP_K, s), dt)
    return pl.pallas_call(
        _router_kernel,
        grid=(s // tt,),
        in_specs=[pl.BlockSpec((tt, d), lambda i: (i, 0)),
                  pl.BlockSpec((n_e, d), lambda i: (0, 0)),
                  pl.BlockSpec((n_e, 1), lambda i: (0, 0))],
        out_specs=[pl.BlockSpec((TOP_K, tt), lambda i: (0, i)),
                   pl.BlockSpec((TOP_K, tt), lambda i: (0, i)),
                   pl.BlockSpec((TOP_K, tt), lambda i: (0, i)),
                   pl.BlockSpec((n_e, 1), lambda i: (0, 0))],
        out_shape=[out3(I32), out3(F32), out3(I32), jax.ShapeDtypeStruct((n_e, 1), I32)],
        scratch_shapes=[pltpu.VMEM((n_e, 1), F32)],
        compiler_params=_params(1),
        name="router_topk",
    )(h1, wr_t, rb_col)


def _finalize_kernel(cnt_ref, e_ref, r_ref, dest_ref, pstart_ref, blk_ref, *, rows, n_blocks):
    n_e = cnt_ref.shape[0]

    def scan(e, start):
        pstart_ref[e] = start
        nblk = (cnt_ref[e] + rows - 1) // rows
        b0 = start // rows

        def fill(j, c):
            blk_ref[b0 + j] = e
            return c

        lax.fori_loop(0, nblk, fill, 0)
        return start + nblk * rows

    end = lax.fori_loop(0, n_e, scan, 0)
    pstart_ref[n_e] = end

    def tail(b, c):
        blk_ref[b] = n_e - 1
        return c

    lax.fori_loop(end // rows, n_blocks, tail, 0)

    chunk = 1024
    for c in range(e_ref.shape[1] // chunk):
        ev = e_ref[:, c * chunk:(c + 1) * chunk]

        def add(e, acc):
            return jnp.where(ev == e, pstart_ref[e], acc)

        base = lax.fori_loop(0, n_e, add, jnp.zeros(ev.shape, I32))
        dest_ref[:, c * chunk:(c + 1) * chunk] = base + r_ref[:, c * chunk:(c + 1) * chunk]


def _finalize(counts, e_t, r_t, rows, n_blocks):
    n_e = counts.shape[0]
    smem = pl.BlockSpec(memory_space=pltpu.SMEM)
    vmem = pl.BlockSpec(memory_space=pltpu.VMEM)
    return pl.pallas_call(
        functools.partial(_finalize_kernel, rows=rows, n_blocks=n_blocks),
        in_specs=[smem, vmem, vmem],
        out_specs=[vmem, smem, smem],
        out_shape=[jax.ShapeDtypeStruct(e_t.shape, I32),
                   jax.ShapeDtypeStruct((n_e + 1,), I32),
                   jax.ShapeDtypeStruct((n_blocks,), I32)],
        compiler_params=pltpu.CompilerParams(vmem_limit_bytes=V7X_VMEM_LIMIT),
        name="route_finalize",
    )(counts, e_t, r_t)


def _dispatch_kernel(cnt_ref, pstart_ref, dest_hbm, h_hbm, xs_hbm, idx_smem, zero_scr, idx_sem, row_sem, pad_sem,
                     *, tile, total_rows):
    i = pl.program_id(0)
    nt = pl.num_programs(0)
    n_e = cnt_ref.shape[0]
    per_tile = tile * TOP_K

    def row_copy(t, dst):
        return pltpu.make_async_copy(h_hbm.at[pl.ds(t, 1)], xs_hbm.at[pl.ds(dst, 1)], row_sem)

    def pad_copy(dst):
        return pltpu.make_async_copy(zero_scr, xs_hbm.at[pl.ds(dst, 1)], pad_sem)

    @pl.when(i == 0)
    def _():
        zero_scr[...] = jnp.zeros(zero_scr.shape, zero_scr.dtype)

        def per_expert(e, c):
            lo = pstart_ref[e] + cnt_ref[e]
            hi = jnp.where(e == n_e - 1, total_rows, pstart_ref[e + 1])

            def start(r, c2):
                pad_copy(r).start()
                return c2

            lax.fori_loop(lo, hi, start, 0)

            def wait(r, c2):
                pad_copy(r).wait()
                return c2

            lax.fori_loop(lo, hi, wait, 0)
            return c

        lax.fori_loop(0, n_e, per_expert, 0)

    idx_cp = pltpu.make_async_copy(dest_hbm.at[pl.ds(i * per_tile, per_tile)], idx_smem, idx_sem)
    idx_cp.start()
    idx_cp.wait()

    def issue(t, c):
        for k in range(TOP_K):
            row_copy(i * tile + t, idx_smem[t * TOP_K + k]).start()
        return c

    lax.fori_loop(0, tile, issue, 0)

    def drain(t, c):
        for k in range(TOP_K):
            row_copy(0, 0).wait()
        return c

    lax.fori_loop(0, tile, drain, 0)
    del nt


def _dispatch(counts, pstart, dest_flat, h1, total_rows, tile):
    s, d = h1.shape
    smem = pl.BlockSpec(memory_space=pltpu.SMEM)
    anyspec = pl.BlockSpec(memory_space=pl.ANY)
    return pl.pallas_call(
        functools.partial(_dispatch_kernel, tile=tile, total_rows=total_rows),
        grid=(s // tile,),
        in_specs=[smem, smem, anyspec, anyspec],
        out_specs=anyspec,
        out_shape=jax.ShapeDtypeStruct((total_rows, d), h1.dtype),
        scratch_shapes=[pltpu.SMEM((tile * TOP_K,), I32),
                        pltpu.VMEM((1, d), h1.dtype),
                        pltpu.SemaphoreType.DMA, pltpu.SemaphoreType.DMA, pltpu.SemaphoreType.DMA],
        compiler_params=_params(1),
        name="moe_dispatch",
    )(counts, pstart, dest_flat, h1)


def _expert_kernel(blk_ref, xs_ref, wg_ref, wu_ref, wd_ref, ys_ref):
    del blk_ref
    x = xs_ref[...].astype(BF16)
    g = jnp.dot(x, wg_ref[...], preferred_element_type=F32)
    u = jnp.dot(x, wu_ref[...], preferred_element_type=F32)
    hid = (jax.nn.silu(g) * u).astype(BF16)
    ys_ref[...] = jnp.dot(hid, wd_ref[...], preferred_element_type=F32)


def _experts(blk_e, xs, wg, wu, wd, rows):
    p, d = xs.shape
    f = wg.shape[2]
    grid_spec = pltpu.PrefetchScalarGridSpec(
        num_scalar_prefetch=1,
        grid=(p // rows,),
        in_specs=[pl.BlockSpec((rows, d), lambda b, be: (b, 0)),
                  pl.BlockSpec((None, d, f), lambda b, be: (be[b], 0, 0)),
                  pl.BlockSpec((None, d, f), lambda b, be: (be[b], 0, 0)),
                  pl.BlockSpec((None, f, d), lambda b, be: (be[b], 0, 0))],
        out_specs=pl.BlockSpec((rows, d), lambda b, be: (b, 0)),
    )
    return pl.pallas_call(
        _expert_kernel,
        grid_spec=grid_spec,
        out_shape=jax.ShapeDtypeStruct((p, d), F32),
        compiler_params=_params(1),
        name="moe_experts",
    )(blk_e, xs, wg, wu, wd)


def _combine_kernel(dest_hbm, ys_hbm, h_ref, w_ref, sg_ref, su_ref, sd_ref, lg_ref, lb_ref, o_ref,
                    idx_smem, gbuf, idx_sem, row_sem, *, tile, alpha):
    i = pl.program_id(0)
    per_tile = tile * TOP_K

    idx_cp = pltpu.make_async_copy(dest_hbm.at[pl.ds(i * per_tile, per_tile)], idx_smem, idx_sem)
    idx_cp.start()
    idx_cp.wait()

    def row_copy(src, k, t):
        return pltpu.make_async_copy(ys_hbm.at[pl.ds(src, 1)], gbuf.at[k, pl.ds(t, 1)], row_sem)

    def issue(t, c):
        for k in range(TOP_K):
            row_copy(idx_smem[t * TOP_K + k], k, t).start()
        return c

    lax.fori_loop(0, tile, issue, 0)

    h = h_ref[...]
    hb = h.astype(BF16)
    g = jnp.dot(hb, sg_ref[...], preferred_element_type=F32)
    u = jnp.dot(hb, su_ref[...], preferred_element_type=F32)
    hid = (jax.nn.silu(g) * u).astype(BF16)
    shared = jnp.dot(hid, sd_ref[...], preferred_element_type=F32)

    def drain(t, c):
        for k in range(TOP_K):
            row_copy(0, k, t).wait()
        return c

    lax.fori_loop(0, tile, drain, 0)

    w = w_ref[...]
    routed = w[:, 0:1] * gbuf[0]
    for k in range(1, TOP_K):
        routed = routed + w[:, k:k + 1] * gbuf[k]
    o_ref[...] = _layer_norm(alpha * h + (routed + shared), lg_ref[...], lb_ref[...])


def _combine(dest_flat, ys, h1, w_tok, sg, su, sd, ln_g, ln_b, alpha, tile):
    s, d = h1.shape
    f = sg.shape[1]
    anyspec = pl.BlockSpec(memory_space=pl.ANY)

    def whole(shape):
        return pl.BlockSpec(shape, lambda i: (0,) * len(shape), pipeline_mode=pl.Buffered(1))

    return pl.pallas_call(
        functools.partial(_combine_kernel, tile=tile, alpha=alpha),
        grid=(s // tile,),
        in_specs=[anyspec, anyspec,
                  pl.BlockSpec((tile, d), lambda i: (i, 0)),
                  pl.BlockSpec((tile, TOP_K), lambda i: (i, 0)),
                  whole((d, f)), whole((d, f)), whole((f, d)),
                  whole((1, d)), whole((1, d))],
        out_specs=pl.BlockSpec((tile, d), lambda i: (i, 0)),
        out_shape=jax.ShapeDtypeStruct((s, d), F32),
        scratch_shapes=[pltpu.SMEM((tile * TOP_K,), I32),
                        pltpu.VMEM((TOP_K, tile, d), F32),
                        pltpu.SemaphoreType.DMA, pltpu.SemaphoreType.DMA],
        compiler_params=_params(1),
        name="moe_combine_ln2",
    )(dest_flat, ys, h1, w_tok, sg, su, sd, ln_g, ln_b)


def _t5_bucket(rel):
    nb = N_BUCKETS // 2
    ret = jnp.where(rel > 0, nb, 0)
    n = jnp.abs(rel)
    max_exact = nb // 2
    nf = jnp.maximum(n, 1).astype(F32)
    large = max_exact + (jnp.log(nf / max_exact) / math.log(MAX_DISTANCE / max_exact) * (nb - max_exact)).astype(I32)
    large = jnp.minimum(large, nb - 1)
    return ret + jnp.where(n < max_exact, n, large)


def _window_bucket_map():
    qi = jnp.arange(BLOCK)[:, None]
    c = jnp.arange(3 * BLOCK)[None, :]
    rel = c - BLOCK - qi
    return jnp.where(jnp.abs(rel) <= WINDOW, _t5_bucket(rel), -1).astype(I32)


def _rope_tables(s):
    half = HEAD_DIM // 2
    qd = half // 2
    pos = jnp.arange(s)
    row = (pos // GRID_W).astype(F32)
    col = (pos % GRID_W).astype(F32)
    inv = ROPE_THETA ** (-jnp.arange(0, half, 2, dtype=F32) / half)
    ang_r = row[:, None] * inv
    ang_c = col[:, None] * inv
    cr, sr, cc, sc = jnp.cos(ang_r), jnp.sin(ang_r), jnp.cos(ang_c), jnp.sin(ang_c)
    z = jnp.zeros((s, qd), F32)
    cos_t = jnp.concatenate([cr, cr, cc, cc], axis=1)
    sa_t = jnp.concatenate([-sr, z, -sc, z], axis=1)
    sb_t = jnp.concatenate([z, sr, z, sc], axis=1)
    return cos_t, sa_t, sb_t


def _layer(h, w_in, b_gate, sink, rel_table, qn_g, kn_g, wba, wbb, w_out, ln1_g, ln1_b, w_router, r_bias,
           weg, weu, wed, wsg, wsu, wsd, ln2_g, ln2_b, alpha):
    s, d = h.shape
    a_q, a_kv = A_HEADS * HEAD_DIM, A_KV_HEADS * HEAD_DIM
    b_q, b_kv = B_HEADS * HEAD_DIM, B_KV_HEADS * HEAD_DIM
    n_qkv = a_q + 2 * a_kv + b_q + 2 * b_kv
    n_e = w_router.shape[1]

    xb = h.astype(BF16)
    w_in_b = w_in.astype(BF16)
    qkv = _matmul(xb, w_in_b[:, :n_qkv], 1024, 1024, BF16, "in_proj_qkv")
    gates = _matmul(xb, w_in_b[:, n_qkv:], 1024, 1024, BF16, "in_proj_gates")

    oa = _window_attention(qkv, rel_table.reshape(-1), sink, _window_bucket_map(), 0, a_q, a_q + a_kv)

    qb0 = a_q + 2 * a_kv
    scale = HEAD_DIM ** -0.5
    gains = jnp.concatenate([jnp.broadcast_to(qn_g * scale, (B_HEADS, HEAD_DIM)),
                             jnp.broadcast_to(kn_g, (B_KV_HEADS, HEAD_DIM))], axis=0)[:, None, :]
    cos_t, sa_t, sb_t = _rope_tables(s)
    qk = _qkprep(qkv, gains, cos_t, sa_t, sb_t, qb0, B_HEADS + B_KV_HEADS, 512)
    ob = _dense_attention(qk, qkv, qb0 + b_q + b_kv, 256, 512)

    h1 = _merge(oa, ob, gates, b_gate, h, wba.astype(BF16), wbb.astype(BF16), w_out.astype(BF16),
                ln1_g[None, :], ln1_b[None, :], alpha, 256)

    e_t, w_t, r_t, counts = _router(h1, w_router.T, r_bias[:, None], ROUTER_TILE)
    rows = EXPERT_ROWS
    total_rows = s * TOP_K + n_e * rows
    n_blocks = total_rows // rows
    counts = counts.reshape(n_e)
    dest_t, pstart, blk_e = _finalize(counts, e_t, r_t, rows, n_blocks)
    dest_flat = dest_t.T.reshape(-1)
    xs = _dispatch(counts, pstart, dest_flat, h1, total_rows, MOVE_TILE)
    ys = _experts(blk_e, xs, weg.astype(BF16), weu.astype(BF16), wed.astype(BF16), rows)
    return _combine(dest_flat, ys, h1, w_t.T, wsg.astype(BF16), wsu.astype(BF16), wsd.astype(BF16),
                    ln2_g[None, :], ln2_b[None, :], alpha, MOVE_TILE)


def kernel(x, w_in, b_gate, attn_sink, rel_bias_table, q_norm_g, k_norm_g, w_branch_a, w_branch_b, w_out, ln1_g, ln1_b, w_router, router_bias, w_exp_gate, w_exp_up, w_exp_down, w_sh_gate, w_sh_up, w_sh_down, ln2_g, ln2_b):
    bsz, s, d = x.shape
    depth = w_in.shape[0]
    alpha = (2 * depth) ** 0.25
    outs = []
    for b in range(bsz):
        h = x[b]
        for l in range(depth):
            h = _layer(h, w_in[l], b_gate[l], attn_sink[l], rel_bias_table, q_norm_g[l], k_norm_g[l],
                       w_branch_a[l], w_branch_b[l], w_out[l], ln1_g[l], ln1_b[l], w_router[l], router_bias[l],
                       w_exp_gate[l], w_exp_up[l], w_exp_down[l], w_sh_gate[l], w_sh_up[l], w_sh_down[l],
                       ln2_g[l], ln2_b[l], alpha)
        outs.append(h)
    return jnp.stack(outs, axis=0)
```

```python
import functools
import math

import jax
import jax.numpy as jnp
from jax import lax
from jax.experimental import pallas as pl
from jax.experimental.pallas import tpu as pltpu

F32 = jnp.float32
BF16 = jnp.bfloat16
I32 = jnp.int32

HEAD_DIM = 128
A_HEADS = 8
A_KV_HEADS = 2
B_HEADS = 8
B_KV_HEADS = 2
WINDOW = 128
BLOCK = 128
N_BUCKETS = 32
MAX_DISTANCE = 128
GRID_W = 64
ROPE_THETA = 10000.0
N_GROUPS = 8
TOPK_GROUPS = 4
TOP_K = 8
ROUTED_SCALE = 2.5
LN_EPS = 1e-5
RMS_EPS = 1e-6
NEG_BIG = -1e30

V7X_VMEM_LIMIT = 56 * 1024 * 1024
EXPERT_ROWS = 256
ROUTER_TILE = 256
MOVE_TILE = 128
DENSE_FAST_BOUND = 50.0


def _params(n_axes):
    return pltpu.CompilerParams(dimension_semantics=("arbitrary",) * n_axes,
                                vmem_limit_bytes=V7X_VMEM_LIMIT)


def _mm_kernel(x_ref, w_ref, o_ref):
    o_ref[...] = jnp.dot(x_ref[...], w_ref[...], preferred_element_type=F32).astype(o_ref.dtype)


def _matmul(x, w, tm, tn, out_dtype, name):
    m, k = x.shape
    n = w.shape[1]
    return pl.pallas_call(
        _mm_kernel,
        grid=(m // tm, n // tn),
        in_specs=[pl.BlockSpec((tm, k), lambda i, j: (i, 0)),
                  pl.BlockSpec((k, tn), lambda i, j: (0, j))],
        out_specs=pl.BlockSpec((tm, tn), lambda i, j: (i, j)),
        out_shape=jax.ShapeDtypeStruct((m, n), out_dtype),
        compiler_params=_params(2),
        name=name,
    )(x, w)


def _norm_rope(x_ref, g_ref, cos_ref, sa_ref, sb_ref):
    x = x_ref[...].astype(F32)
    ms = jnp.mean(x * x, axis=-1, keepdims=True)
    y = x * lax.rsqrt(ms + RMS_EPS) * g_ref[...]
    out = y * cos_ref[...] + pltpu.roll(y, 96, 1) * sa_ref[...] + pltpu.roll(y, 32, 1) * sb_ref[...]
    return out, jnp.sum(y * y, axis=-1, keepdims=True)


def _lane0(val):
    lane = lax.broadcasted_iota(I32, val.shape, 1)
    return jnp.where(lane == 0, val, 0.0)


def _kprep_kernel(k_ref, v_ref, g_ref, cos_ref, sa_ref, sb_ref, ko_ref, vo_ref, kmax_ref):
    i = pl.program_id(1)
    k, ss = _norm_rope(k_ref, g_ref, cos_ref, sa_ref, sb_ref)
    ones0 = _lane0(jnp.ones(k.shape, F32))
    ko_ref[...] = jnp.concatenate([k, ones0], axis=1).astype(ko_ref.dtype)
    vo_ref[...] = jnp.concatenate([v_ref[...], ones0.astype(v_ref.dtype)], axis=1)
    tile_max = jnp.broadcast_to(jnp.max(ss, axis=0, keepdims=True), kmax_ref.shape)

    @pl.when(i == 0)
    def _():
        kmax_ref[...] = tile_max

    @pl.when(i > 0)
    def _():
        kmax_ref[...] = jnp.maximum(kmax_ref[...], tile_max)


def _kprep(qkv, gain, cos_t, sa_t, sb_t, k_col0, v_col0, tr):
    s = qkv.shape[0]
    kc, vc = k_col0 // HEAD_DIM, v_col0 // HEAD_DIM
    tab = pl.BlockSpec((tr, HEAD_DIM), lambda h, i: (i, 0))
    return pl.pallas_call(
        _kprep_kernel,
        grid=(B_KV_HEADS, s // tr),
        in_specs=[pl.BlockSpec((tr, HEAD_DIM), lambda h, i: (i, kc + h)),
                  pl.BlockSpec((tr, HEAD_DIM), lambda h, i: (i, vc + h)),
                  pl.BlockSpec((1, HEAD_DIM), lambda h, i: (0, 0)),
                  tab, tab, tab],
        out_specs=[pl.BlockSpec((tr, 2 * HEAD_DIM), lambda h, i: (i, h)),
                   pl.BlockSpec((tr, 2 * HEAD_DIM), lambda h, i: (i, h)),
                   pl.BlockSpec((None, 1, HEAD_DIM), lambda h, i: (h, 0, 0))],
        out_shape=[jax.ShapeDtypeStruct((s, B_KV_HEADS * 2 * HEAD_DIM), BF16),
                   jax.ShapeDtypeStruct((s, B_KV_HEADS * 2 * HEAD_DIM), BF16),
                   jax.ShapeDtypeStruct((B_KV_HEADS, 1, HEAD_DIM), F32)],
        compiler_params=_params(2),
        name="k_prep",
    )(qkv, qkv, gain, cos_t, sa_t, sb_t)


def _qprep_kernel(q_ref, g_ref, cos_ref, sa_ref, sb_ref, kmax_ref, qo_ref, mmax_ref):
    i = pl.program_id(1)
    q, ss = _norm_rope(q_ref, g_ref, cos_ref, sa_ref, sb_ref)
    m = jnp.sqrt(ss) * jnp.sqrt(kmax_ref[...])
    qo_ref[...] = jnp.concatenate([q, _lane0(-m)], axis=1).astype(qo_ref.dtype)
    tile_max = jnp.max(m, axis=0, keepdims=True)

    @pl.when(i == 0)
    def _():
        mmax_ref[...] = tile_max

    @pl.when(i > 0)
    def _():
        mmax_ref[...] = jnp.maximum(mmax_ref[...], tile_max)


def _qprep(qkv, gain, cos_t, sa_t, sb_t, kmax, q_col0, tr):
    s = qkv.shape[0]
    qc = q_col0 // HEAD_DIM
    g_per = B_HEADS // B_KV_HEADS
    tab = pl.BlockSpec((tr, HEAD_DIM), lambda h, i: (i, 0))
    return pl.pallas_call(
        _qprep_kernel,
        grid=(B_HEADS, s // tr),
        in_specs=[pl.BlockSpec((tr, HEAD_DIM), lambda h, i: (i, qc + h)),
                  pl.BlockSpec((1, HEAD_DIM), lambda h, i: (0, 0)),
                  tab, tab, tab,
                  pl.BlockSpec((None, 1, HEAD_DIM), lambda h, i: (h // g_per, 0, 0))],
        out_specs=[pl.BlockSpec((tr, 2 * HEAD_DIM), lambda h, i: (i, h)),
                   pl.BlockSpec((None, 1, HEAD_DIM), lambda h, i: (h, 0, 0))],
        out_shape=[jax.ShapeDtypeStruct((s, B_HEADS * 2 * HEAD_DIM), BF16),
                   jax.ShapeDtypeStruct((B_HEADS, 1, HEAD_DIM), F32)],
        compiler_params=_params(2),
        name="q_prep",
    )(qkv, gain, cos_t, sa_t, sb_t, kmax)


def _window_kernel(tab_ref, sink_ref, bucket_ref, q_ref, kp_ref, kc_ref, kn_ref, vp_ref, vc_ref, vn_ref,
                   o_ref, bias_scr):
    kvh = pl.program_id(0)
    n = pl.program_id(1)
    nb = pl.num_programs(1)
    g_per = A_HEADS // A_KV_HEADS

    log2e = math.log2(math.e)

    @pl.when((kvh == 0) & (n == 0))
    def _():
        bucket = bucket_ref[...]
        col = lax.broadcasted_iota(I32, bucket.shape, 1)
        for h in range(A_HEADS):
            acc = jnp.full(bucket.shape, NEG_BIG, F32)
            for b in range(N_BUCKETS):
                acc = jnp.where(bucket == b, tab_ref[b * A_HEADS + h] * log2e, acc)
            bias_scr[0, h] = acc
            bias_scr[1, h] = jnp.where(col < BLOCK, NEG_BIG, acc)
            bias_scr[2, h] = jnp.where(col >= 2 * BLOCK, NEG_BIG, acc)

    q = q_ref[...]
    q4 = jnp.concatenate([q[:, g * HEAD_DIM:(g + 1) * HEAD_DIM] for g in range(g_per)], axis=0)
    kcat = jnp.concatenate([kp_ref[...], kc_ref[...], kn_ref[...]], axis=0)
    vcat = jnp.concatenate([vp_ref[...], vc_ref[...], vn_ref[...]], axis=0)
    lane = lax.broadcasted_iota(I32, vcat.shape, 1)
    vext = jnp.concatenate([vcat, jnp.where(lane == 0, 1.0, 0.0).astype(vcat.dtype)], axis=1)
    case = jnp.where(n == 0, 1, jnp.where(n == nb - 1, 2, 0))
    bias4 = bias_scr[case, pl.ds(kvh * g_per, g_per)]
    s = lax.dot_general(q4, kcat, (((1,), (1,)), ((), ())), preferred_element_type=F32)
    s = s + bias4.reshape(g_per * BLOCK, 3 * BLOCK)
    sk = jnp.concatenate([jnp.full((BLOCK, 1), sink_ref[kvh * g_per + g] * log2e, F32) for g in range(g_per)],
                         axis=0)
    m = jnp.maximum(jnp.max(s, axis=-1, keepdims=True), sk)
    p = jnp.exp2(s - m)
    oe = jnp.dot(p.astype(BF16), vext, preferred_element_type=F32)
    o = oe[:, :HEAD_DIM] / (oe[:, HEAD_DIM:HEAD_DIM + 1] + jnp.exp2(sk - m))
    o_ref[...] = jnp.concatenate([o[g * BLOCK:(g + 1) * BLOCK] for g in range(g_per)], axis=1).astype(o_ref.dtype)


def _window_attention(qkv, tab_flat, sink, bucket, q_col0, k_col0, v_col0):
    s = qkv.shape[0]
    nb = s // BLOCK
    assert nb >= 2, "the boundary-case bias tables assume distinct first and last blocks"
    g_per = A_HEADS // A_KV_HEADS
    qw = g_per * HEAD_DIM
    qc, kc, vc = q_col0 // qw, k_col0 // HEAD_DIM, v_col0 // HEAD_DIM

    def kv_spec(c0, shift):
        return pl.BlockSpec((BLOCK, HEAD_DIM),
                            lambda h, n: (jnp.clip(n + shift, 0, nb - 1), c0 + h))

    smem = pl.BlockSpec(memory_space=pltpu.SMEM)
    return pl.pallas_call(
        _window_kernel,
        grid=(A_KV_HEADS, nb),
        in_specs=[smem, smem,
                  pl.BlockSpec((BLOCK, 3 * BLOCK), lambda h, n: (0, 0)),
                  pl.BlockSpec((BLOCK, qw), lambda h, n: (n, qc + h)),
                  kv_spec(kc, -1), kv_spec(kc, 0), kv_spec(kc, 1),
                  kv_spec(vc, -1), kv_spec(vc, 0), kv_spec(vc, 1)],
        out_specs=pl.BlockSpec((BLOCK, qw), lambda h, n: (n, h)),
        out_shape=jax.ShapeDtypeStruct((s, A_HEADS * HEAD_DIM), BF16),
        scratch_shapes=[pltpu.VMEM((3, A_HEADS, BLOCK, 3 * BLOCK), F32)],
        compiler_params=_params(2),
        name="window_attn",
    )(tab_flat, sink, bucket, qkv, qkv, qkv, qkv, qkv, qkv, qkv)


def _stack_heads(q_ref, width):
    g_per = B_HEADS // B_KV_HEADS
    q = q_ref[...]
    ext = 2 * HEAD_DIM
    return jnp.concatenate([q[:, g * ext:g * ext + width] for g in range(g_per)], axis=0)


def _unstack_heads(o, tq):
    g_per = B_HEADS // B_KV_HEADS
    return jnp.concatenate([o[g * tq:(g + 1) * tq] for g in range(g_per)], axis=1)


def _dense_fast_kernel(q_ref, k_ref, v_ref, o_ref, acc_scr, *, tk, unroll):
    tq = q_ref.shape[0]
    s_len = k_ref.shape[0]
    q4 = _stack_heads(q_ref, 2 * HEAD_DIM)
    acc_scr[...] = jnp.zeros(acc_scr.shape, F32)

    def body(c, carry):
        off = pl.multiple_of(c * tk, tk)
        k = k_ref[pl.ds(off, tk), :]
        v = v_ref[pl.ds(off, tk), :]
        s2 = lax.dot_general(q4, k, (((1,), (1,)), ((), ())), preferred_element_type=F32)
        p = jnp.exp2(s2).astype(BF16)
        acc_scr[...] += jnp.dot(p, v, preferred_element_type=F32)
        return carry

    lax.fori_loop(0, s_len // tk, body, 0, unroll=unroll)
    acc = acc_scr[...]
    o = acc[:, :HEAD_DIM] / acc[:, HEAD_DIM:HEAD_DIM + 1]
    o_ref[...] = _unstack_heads(o, tq).astype(o_ref.dtype)


def _dense_exact_kernel(q_ref, k_ref, v_ref, o_ref, m_scr, l_scr, acc_scr, *, tk):
    tq = q_ref.shape[0]
    s_len = k_ref.shape[0]
    q4 = _stack_heads(q_ref, HEAD_DIM)
    m_scr[...] = jnp.full(m_scr.shape, -jnp.inf, F32)
    l_scr[...] = jnp.zeros(l_scr.shape, F32)
    acc_scr[...] = jnp.zeros(acc_scr.shape, F32)

    def body(c, carry):
        off = pl.multiple_of(c * tk, tk)
        k = k_ref[pl.ds(off, tk), :HEAD_DIM]
        v = v_ref[pl.ds(off, tk), :HEAD_DIM]
        s2 = lax.dot_general(q4, k, (((1,), (1,)), ((), ())), preferred_element_type=F32)
        m_prev = m_scr[...]
        m_new = jnp.maximum(m_prev, jnp.max(s2, axis=-1, keepdims=True))
        a = jnp.exp2(m_prev - m_new)
        p = jnp.exp2(s2 - m_new)
        l_scr[...] = a * l_scr[...] + jnp.sum(p, axis=-1, keepdims=True)
        acc_scr[...] = a * acc_scr[...] + jnp.dot(p.astype(BF16), v, preferred_element_type=F32)
        m_scr[...] = m_new
        return carry

    lax.fori_loop(0, s_len // tk, body, 0)
    o = acc_scr[...] / l_scr[...]
    o_ref[...] = _unstack_heads(o, tq).astype(o_ref.dtype)


def _dense_call(kernel_fn, scratch, name, q_ext, k_ext, v_ext, tq):
    s = q_ext.shape[0]
    g_per = B_HEADS // B_KV_HEADS
    ext = 2 * HEAD_DIM
    return pl.pallas_call(
        kernel_fn,
        grid=(B_KV_HEADS, s // tq),
        in_specs=[pl.BlockSpec((tq, g_per * ext), lambda h, i: (i, h)),
                  pl.BlockSpec((s, ext), lambda h, i: (0, h)),
                  pl.BlockSpec((s, ext), lambda h, i: (0, h))],
        out_specs=pl.BlockSpec((tq, g_per * HEAD_DIM), lambda h, i: (i, h)),
        out_shape=jax.ShapeDtypeStruct((s, B_HEADS * HEAD_DIM), BF16),
        scratch_shapes=scratch,
        compiler_params=_params(2),
        name=name,
    )(q_ext, k_ext, v_ext)


def _dense_attention(q_ext, k_ext, v_ext, bound):
    g_per = B_HEADS // B_KV_HEADS

    def fast(q, k, v):
        tq = 256
        return _dense_call(functools.partial(_dense_fast_kernel, tk=512, unroll=4),
                           [pltpu.VMEM((g_per * tq, 2 * HEAD_DIM), F32)], "dense_attn_fast", q, k, v, tq)

    def exact(q, k, v):
        tq = 256
        return _dense_call(functools.partial(_dense_exact_kernel, tk=512),
                           [pltpu.VMEM((g_per * tq, 1), F32), pltpu.VMEM((g_per * tq, 1), F32),
                            pltpu.VMEM((g_per * tq, HEAD_DIM), F32)], "dense_attn_exact", q, k, v, tq)

    return lax.cond(bound <= DENSE_FAST_BOUND, fast, exact, q_ext, k_ext, v_ext)


def _layer_norm(y, g, b):
    mu = jnp.mean(y, axis=-1, keepdims=True)
    yc = y - mu
    var = jnp.mean(yc * yc, axis=-1, keepdims=True)
    return yc * lax.rsqrt(var + LN_EPS) * g + b


def _merge_kernel(oa_ref, ob_ref, ga_ref, gb_ref, bg_ref, x_ref, wa_ref, wb_ref, wo_ref, lg_ref, lb_ref,
                  h_ref, *, alpha):
    ma = jnp.dot(oa_ref[...], wa_ref[...], preferred_element_type=F32)
    mb = jnp.dot(ob_ref[...], wb_ref[...], preferred_element_type=F32)
    gate_a = jax.nn.sigmoid(ga_ref[...].astype(F32) + bg_ref[0:1, :])
    gate_b = jax.nn.sigmoid(gb_ref[...].astype(F32) + bg_ref[1:2, :])
    merged = gate_a * ma + gate_b * mb
    mix = jnp.dot(merged.astype(BF16), wo_ref[...], preferred_element_type=F32)
    h_ref[...] = _layer_norm(alpha * x_ref[...] + mix, lg_ref[...], lb_ref[...])


def _merge(oa, ob, gates, b_gate, x2, wa, wb, wo, ln_g, ln_b, alpha, tm):
    s, d = x2.shape
    ca = oa.shape[1]

    def whole(shape):
        return pl.BlockSpec(shape, lambda i: (0,) * len(shape), pipeline_mode=pl.Buffered(1))

    return pl.pallas_call(
        functools.partial(_merge_kernel, alpha=alpha),
        grid=(s // tm,),
        in_specs=[pl.BlockSpec((tm, ca), lambda i: (i, 0)),
                  pl.BlockSpec((tm, ca), lambda i: (i, 0)),
                  pl.BlockSpec((tm, d), lambda i: (i, 0)),
                  pl.BlockSpec((tm, d), lambda i: (i, 1)),
                  whole((2, d)),
                  pl.BlockSpec((tm, d), lambda i: (i, 0)),
                  whole((ca, d)), whole((ca, d)), whole((d, d)),
                  whole((1, d)), whole((1, d))],
        out_specs=pl.BlockSpec((tm, d), lambda i: (i, 0)),
        out_shape=jax.ShapeDtypeStruct((s, d), F32),
        compiler_params=_params(1),
        name="merge_ln1",
    )(oa, ob, gates, gates, b_gate, x2, wa, wb, wo, ln_g, ln_b)


def _split_bf16(a):
    hi = a.astype(BF16)
    lo = (a - hi.astype(F32)).astype(BF16)
    return hi, lo


def _router_kernel(h_ref, wr_ref, rb_ref, e_ref, w_ref, r_ref, cnt_ref, carry_scr):
    i = pl.program_id(0)
    tt = h_ref.shape[0]
    n_e = wr_ref.shape[0]
    per = n_e // N_GROUPS

    @pl.when(i == 0)
    def _():
        carry_scr[...] = jnp.zeros(carry_scr.shape, F32)

    h_hi, h_lo = _split_bf16(h_ref[...])
    w_hi, w_lo = _split_bf16(wr_ref[...])
    dn = (((1,), (1,)), ((), ()))
    logits = (lax.dot_general(w_hi, h_hi, dn, preferred_element_type=F32)
              + lax.dot_general(w_hi, h_lo, dn, preferred_element_type=F32)
              + lax.dot_general(w_lo, h_hi, dn, preferred_element_type=F32))
    scores = jax.nn.sigmoid(logits)
    biased = scores + rb_ref[...]

    sc3 = scores.reshape(N_GROUPS, per, tt)
    b3 = biased.reshape(N_GROUPS, per, tt)
    gi = lax.broadcasted_iota(I32, b3.shape, 0)
    ji = lax.broadcasted_iota(I32, b3.shape, 1)
    ei = gi * per + ji

    m1 = jnp.max(b3, axis=1, keepdims=True)
    first = jnp.min(jnp.where(b3 == m1, ji, per), axis=1, keepdims=True)
    m2 = jnp.max(jnp.where(ji == first, -jnp.inf, b3), axis=1, keepdims=True)
    gs = m1 + m2

    g1 = lax.broadcasted_iota(I32, gs.shape, 0)
    gsel = jnp.zeros(gs.shape, F32)
    cur = gs
    for _ in range(TOPK_GROUPS):
        mx = jnp.max(cur, axis=0, keepdims=True)
        fi = jnp.min(jnp.where(cur == mx, g1, N_GROUPS), axis=0, keepdims=True)
        pick = g1 == fi
        gsel = jnp.where(pick, 1.0, gsel)
        cur = jnp.where(pick, -jnp.inf, cur)

    masked = jnp.where(gsel > 0.5, b3, -jnp.inf)
    sel = jnp.zeros(b3.shape, F32)
    picks, e_rows, w_rows = [], [], []
    for _ in range(TOP_K):
        mx = jnp.max(jnp.max(masked, axis=0, keepdims=True), axis=1, keepdims=True)
        fi = jnp.min(jnp.min(jnp.where(masked == mx, ei, n_e), axis=0, keepdims=True), axis=1, keepdims=True)
        pick = ei == fi
        picks.append(pick)
        e_rows.append(fi.reshape(1, tt))
        w_rows.append(jnp.sum(jnp.sum(jnp.where(pick, sc3, 0.0), axis=0, keepdims=True), axis=1,
                              keepdims=True).reshape(1, tt))
        sel = jnp.where(pick, 1.0, sel)
        masked = jnp.where(pick, -jnp.inf, masked)

    wsum = w_rows[0]
    for k in range(1, TOP_K):
        wsum = wsum + w_rows[k]

    sel2 = sel.reshape(n_e, tt)
    rr = lax.broadcasted_iota(I32, (tt, tt), 0)
    cc = lax.broadcasted_iota(I32, (tt, tt), 1)
    upper = jnp.where(rr < cc, 1.0, 0.0).astype(BF16)
    before = jnp.dot(sel2.astype(BF16), upper, preferred_element_type=F32) + carry_scr[...]
    before3 = before.reshape(N_GROUPS, per, tt)
    for k in range(TOP_K):
        rk = jnp.sum(jnp.sum(jnp.where(picks[k], before3, 0.0), axis=0, keepdims=True), axis=1, keepdims=True)
        e_ref[k:k + 1, :] = e_rows[k]
        w_ref[k:k + 1, :] = w_rows[k] / wsum * ROUTED_SCALE
        r_ref[k:k + 1, :] = rk.reshape(1, tt).astype(I32)

    carry_scr[...] = carry_scr[...] + jnp.sum(sel2, axis=1, keepdims=True)
    cnt_ref[...] = carry_scr[...].astype(I32)


def _router(h1, wr_t, rb_col, tt):
    s, d = h1.shape
    n_e = wr_t.shape[0]
    out3 = lambda dt: jax.ShapeDtypeStruct((TOP_K, s), dt)
    return pl.pallas_call(
        _router_kernel,
        grid=(s // tt,),
        in_specs=[pl.BlockSpec((tt, d), lambda i: (i, 0)),
                  pl.BlockSpec((n_e, d), lambda i: (0, 0)),
                  pl.BlockSpec((n_e, 1), lambda i: (0, 0))],
        out_specs=[pl.BlockSpec((TOP_K, tt), lambda i: (0, i)),
                   pl.BlockSpec((TOP_K, tt), lambda i: (0, i)),
                   pl.BlockSpec((TOP_K, tt), lambda i: (0, i)),
                   pl.BlockSpec((n_e, 1), lambda i: (0, 0))],
        out_shape=[out3(I32), out3(F32), out3(I32), jax.ShapeDtypeStruct((n_e, 1), I32)],
        scratch_shapes=[pltpu.VMEM((n_e, 1), F32)],
        compiler_params=_params(1),
        name="router_topk",
    )(h1, wr_t, rb_col)


def _finalize_kernel(cnt_ref, e_ref, r_ref, dest_ref, pstart_ref, blk_ref, *, rows, n_blocks):
    n_e = cnt_ref.shape[0]

    def scan(e, start):
        pstart_ref[e] = start
        nblk = (cnt_ref[e] + rows - 1) // rows
        b0 = start // rows

        def fill(j, c):
            blk_ref[b0 + j] = e
            return c

        lax.fori_loop(0, nblk, fill, 0)
        return start + nblk * rows

    end = lax.fori_loop(0, n_e, scan, 0)
    pstart_ref[n_e] = end

    def tail(b, c):
        blk_ref[b] = n_e - 1
        return c

    lax.fori_loop(end // rows, n_blocks, tail, 0)

    chunk = 1024
    for c in range(e_ref.shape[1] // chunk):
        ev = e_ref[:, c * chunk:(c + 1) * chunk]

        def add(e, acc):
            return jnp.where(ev == e, pstart_ref[e], acc)

        base = lax.fori_loop(0, n_e, add, jnp.zeros(ev.shape, I32))
        dest_ref[:, c * chunk:(c + 1) * chunk] = base + r_ref[:, c * chunk:(c + 1) * chunk]


def _finalize(counts, e_t, r_t, rows, n_blocks):
    n_e = counts.shape[0]
    smem = pl.BlockSpec(memory_space=pltpu.SMEM)
    vmem = pl.BlockSpec(memory_space=pltpu.VMEM)
    return pl.pallas_call(
        functools.partial(_finalize_kernel, rows=rows, n_blocks=n_blocks),
        in_specs=[smem, vmem, vmem],
        out_specs=[vmem, smem, smem],
        out_shape=[jax.ShapeDtypeStruct(e_t.shape, I32),
                   jax.ShapeDtypeStruct((n_e + 1,), I32),
                   jax.ShapeDtypeStruct((n_blocks,), I32)],
        compiler_params=pltpu.CompilerParams(vmem_limit_bytes=V7X_VMEM_LIMIT),
        name="route_finalize",
    )(counts, e_t, r_t)


def _dispatch_kernel(cnt_ref, pstart_ref, dest_hbm, h_ref, xs_hbm, idx_smem, zero_scr, idx_sem, row_sem, pad_sem,
                     *, tile, total_rows):
    i = pl.program_id(0)
    n_e = cnt_ref.shape[0]
    per_tile = tile * TOP_K

    def row_copy(t, dst):
        return pltpu.make_async_copy(h_ref.at[pl.ds(t, 1)], xs_hbm.at[pl.ds(dst, 1)], row_sem)

    def pad_copy(dst):
        return pltpu.make_async_copy(zero_scr, xs_hbm.at[pl.ds(dst, 1)], pad_sem)

    @pl.when(i == 0)
    def _():
        zero_scr[...] = jnp.zeros(zero_scr.shape, zero_scr.dtype)

        def per_expert(e, c):
            lo = pstart_ref[e] + cnt_ref[e]
            hi = jnp.where(e == n_e - 1, total_rows, pstart_ref[e + 1])

            def start(r, c2):
                pad_copy(r).start()
                return c2

            lax.fori_loop(lo, hi, start, 0)

            def wait(r, c2):
                pad_copy(r).wait()
                return c2

            lax.fori_loop(lo, hi, wait, 0)
            return c

        lax.fori_loop(0, n_e, per_expert, 0)

    idx_cp = pltpu.make_async_copy(dest_hbm.at[pl.ds(i * per_tile, per_tile)], idx_smem, idx_sem)
    idx_cp.start()
    idx_cp.wait()

    def issue(t, c):
        for k in range(TOP_K):
            row_copy(t, idx_smem[t * TOP_K + k]).start()
        return c

    lax.fori_loop(0, tile, issue, 0)

    def drain(t, c):
        for k in range(TOP_K):
            row_copy(0, 0).wait()
        return c

    lax.fori_loop(0, tile, drain, 0)


def _dispatch(counts, pstart, dest_flat, h1, total_rows, tile):
    s, d = h1.shape
    smem = pl.BlockSpec(memory_space=pltpu.SMEM)
    anyspec = pl.BlockSpec(memory_space=pl.ANY)
    return pl.pallas_call(
        functools.partial(_dispatch_kernel, tile=tile, total_rows=total_rows),
        grid=(s // tile,),
        in_specs=[smem, smem, anyspec, pl.BlockSpec((tile, d), lambda i: (i, 0))],
        out_specs=anyspec,
        out_shape=jax.ShapeDtypeStruct((total_rows, d), h1.dtype),
        scratch_shapes=[pltpu.SMEM((tile * TOP_K,), I32),
                        pltpu.VMEM((1, d), h1.dtype),
                        pltpu.SemaphoreType.DMA, pltpu.SemaphoreType.DMA, pltpu.SemaphoreType.DMA],
        compiler_params=_params(1),
        name="moe_dispatch",
    )(counts, pstart, dest_flat, h1)


def _expert_kernel(blk_ref, pstart_ref, xs_ref, wg_ref, wu_ref, wd_ref, ys_ref, wg_s, wu_s, wd_s):
    b = pl.program_id(0)
    rows = xs_ref.shape[0]
    n_used = pstart_ref[pstart_ref.shape[0] - 1] // rows
    prev = blk_ref[jnp.maximum(b - 1, 0)]

    @pl.when((b == 0) | (blk_ref[b] != prev))
    def _():
        wg_s[...] = wg_ref[...].astype(BF16)
        wu_s[...] = wu_ref[...].astype(BF16)
        wd_s[...] = wd_ref[...].astype(BF16)

    @pl.when(b < n_used)
    def _():
        x = xs_ref[...].astype(BF16)
        g = jnp.dot(x, wg_s[...], preferred_element_type=F32)
        u = jnp.dot(x, wu_s[...], preferred_element_type=F32)
        hid = (jax.nn.silu(g) * u).astype(BF16)
        ys_ref[...] = jnp.dot(hid, wd_s[...], preferred_element_type=F32)

    @pl.when(b >= n_used)
    def _():
        ys_ref[...] = jnp.zeros(ys_ref.shape, ys_ref.dtype)


def _experts(blk_e, pstart, xs, wg, wu, wd, rows):
    p, d = xs.shape
    f = wg.shape[2]
    grid_spec = pltpu.PrefetchScalarGridSpec(
        num_scalar_prefetch=2,
        grid=(p // rows,),
        in_specs=[pl.BlockSpec((rows, d), lambda b, be, ps: (b, 0)),
                  pl.BlockSpec((None, d, f), lambda b, be, ps: (be[b], 0, 0)),
                  pl.BlockSpec((None, d, f), lambda b, be, ps: (be[b], 0, 0)),
                  pl.BlockSpec((None, f, d), lambda b, be, ps: (be[b], 0, 0))],
        out_specs=pl.BlockSpec((rows, d), lambda b, be, ps: (b, 0)),
        scratch_shapes=[pltpu.VMEM((d, f), BF16), pltpu.VMEM((d, f), BF16), pltpu.VMEM((f, d), BF16)],
    )
    return pl.pallas_call(
        _expert_kernel,
        grid_spec=grid_spec,
        out_shape=jax.ShapeDtypeStruct((p, d), F32),
        compiler_params=_params(1),
        name="moe_experts",
    )(blk_e, pstart, xs, wg, wu, wd)


def _combine_kernel(dest_hbm, ys_hbm, h_ref, w_ref, sg_ref, su_ref, sd_ref, lg_ref, lb_ref, o_ref,
                    idx_smem, gbuf, idx_sem, row_sem, *, tile, alpha):
    i = pl.program_id(0)
    per_tile = tile * TOP_K

    idx_cp = pltpu.make_async_copy(dest_hbm.at[pl.ds(i * per_tile, per_tile)], idx_smem, idx_sem)
    idx_cp.start()
    idx_cp.wait()

    def row_copy(src, k, t):
        return pltpu.make_async_copy(ys_hbm.at[pl.ds(src, 1)], gbuf.at[k, pl.ds(t, 1)], row_sem)

    def issue(t, c):
        for k in range(TOP_K):
            row_copy(idx_smem[t * TOP_K + k], k, t).start()
        return c

    lax.fori_loop(0, tile, issue, 0)

    h = h_ref[...]
    hb = h.astype(BF16)
    g = jnp.dot(hb, sg_ref[...], preferred_element_type=F32)
    u = jnp.dot(hb, su_ref[...], preferred_element_type=F32)
    hid = (jax.nn.silu(g) * u).astype(BF16)
    shared = jnp.dot(hid, sd_ref[...], preferred_element_type=F32)

    def drain(t, c):
        for k in range(TOP_K):
            row_copy(0, k, t).wait()
        return c

    lax.fori_loop(0, tile, drain, 0)

    w = w_ref[...]
    routed = w[:, 0:1] * gbuf[0]
    for k in range(1, TOP_K):
        routed = routed + w[:, k:k + 1] * gbuf[k]
    o_ref[...] = _layer_norm(alpha * h + (routed + shared), lg_ref[...], lb_ref[...])


def _combine(dest_flat, ys, h1, w_tok, sg, su, sd, ln_g, ln_b, alpha, tile):
    s, d = h1.shape
    f = sg.shape[1]
    anyspec = pl.BlockSpec(memory_space=pl.ANY)

    def whole(shape):
        return pl.BlockSpec(shape, lambda i: (0,) * len(shape), pipeline_mode=pl.Buffered(1))

    return pl.pallas_call(
        functools.partial(_combine_kernel, tile=tile, alpha=alpha),
        grid=(s // tile,),
        in_specs=[anyspec, anyspec,
                  pl.BlockSpec((tile, d), lambda i: (i, 0)),
                  pl.BlockSpec((tile, TOP_K), lambda i: (i, 0)),
                  whole((d, f)), whole((d, f)), whole((f, d)),
                  whole((1, d)), whole((1, d))],
        out_specs=pl.BlockSpec((tile, d), lambda i: (i, 0)),
        out_shape=jax.ShapeDtypeStruct((s, d), F32),
        scratch_shapes=[pltpu.SMEM((tile * TOP_K,), I32),
                        pltpu.VMEM((TOP_K, tile, d), F32),
                        pltpu.SemaphoreType.DMA, pltpu.SemaphoreType.DMA],
        compiler_params=_params(1),
        name="moe_combine_ln2",
    )(dest_flat, ys, h1, w_tok, sg, su, sd, ln_g, ln_b)


def _t5_bucket(rel):
    nb = N_BUCKETS // 2
    ret = jnp.where(rel > 0, nb, 0)
    n = jnp.abs(rel)
    max_exact = nb // 2
    nf = jnp.maximum(n, 1).astype(F32)
    large = max_exact + (jnp.log(nf / max_exact) / math.log(MAX_DISTANCE / max_exact) * (nb - max_exact)).astype(I32)
    large = jnp.minimum(large, nb - 1)
    return ret + jnp.where(n < max_exact, n, large)


def _window_bucket_map():
    qi = jnp.arange(BLOCK)[:, None]
    c = jnp.arange(3 * BLOCK)[None, :]
    rel = c - BLOCK - qi
    return jnp.where(jnp.abs(rel) <= WINDOW, _t5_bucket(rel), -1).astype(I32)


def _rope_tables(s):
    half = HEAD_DIM // 2
    qd = half // 2
    pos = jnp.arange(s)
    row = (pos // GRID_W).astype(F32)
    col = (pos % GRID_W).astype(F32)
    inv = ROPE_THETA ** (-jnp.arange(0, half, 2, dtype=F32) / half)
    ang_r = row[:, None] * inv
    ang_c = col[:, None] * inv
    cr, sr, cc, sc = jnp.cos(ang_r), jnp.sin(ang_r), jnp.cos(ang_c), jnp.sin(ang_c)
    z = jnp.zeros((s, qd), F32)
    cos_t = jnp.concatenate([cr, cr, cc, cc], axis=1)
    sa_t = jnp.concatenate([-sr, z, -sc, z], axis=1)
    sb_t = jnp.concatenate([z, sr, z, sc], axis=1)
    return cos_t, sa_t, sb_t


def _layer(h, w_in, b_gate, sink, rel_table, qn_g, kn_g, wba, wbb, w_out, ln1_g, ln1_b, w_router, r_bias,
           weg, weu, wed, wsg, wsu, wsd, ln2_g, ln2_b, alpha):
    s, d = h.shape
    a_q, a_kv = A_HEADS * HEAD_DIM, A_KV_HEADS * HEAD_DIM
    b_q, b_kv = B_HEADS * HEAD_DIM, B_KV_HEADS * HEAD_DIM
    n_qkv = a_q + 2 * a_kv + b_q + 2 * b_kv
    n_e = w_router.shape[1]

    xb = h.astype(BF16)
    qa_scale = jnp.where(jnp.arange(w_in.shape[1]) < a_q, HEAD_DIM ** -0.5 * math.log2(math.e), 1.0)
    w_in_b = (w_in * qa_scale.astype(F32)).astype(BF16)
    qkv = _matmul(xb, w_in_b[:, :n_qkv], 1024, 1024, BF16, "in_proj_qkv")
    gates = _matmul(xb, w_in_b[:, n_qkv:], 1024, 1024, BF16, "in_proj_gates")

    oa = _window_attention(qkv, rel_table.reshape(-1), sink, _window_bucket_map(), 0, a_q, a_q + a_kv)

    qb0 = a_q + 2 * a_kv
    q_gain = (qn_g * (HEAD_DIM ** -0.5 * math.log2(math.e)))[None, :]
    cos_t, sa_t, sb_t = _rope_tables(s)
    k_ext, v_ext, kmax = _kprep(qkv, kn_g[None, :], cos_t, sa_t, sb_t, qb0 + b_q, qb0 + b_q + b_kv, 512)
    q_ext, mmax = _qprep(qkv, q_gain, cos_t, sa_t, sb_t, kmax, qb0, 512)
    ob = _dense_attention(q_ext, k_ext, v_ext, jnp.max(mmax))

    h1 = _merge(oa, ob, gates, b_gate, h, wba.astype(BF16), wbb.astype(BF16), w_out.astype(BF16),
                ln1_g[None, :], ln1_b[None, :], alpha, 256)

    e_t, w_t, r_t, counts = _router(h1, w_router.T, r_bias[:, None], ROUTER_TILE)
    rows = EXPERT_ROWS
    total_rows = s * TOP_K + n_e * rows
    n_blocks = total_rows // rows
    counts = counts.reshape(n_e)
    dest_t, pstart, blk_e = _finalize(counts, e_t, r_t, rows, n_blocks)
    dest_flat = dest_t.T.reshape(-1)
    xs = _dispatch(counts, pstart, dest_flat, h1, total_rows, MOVE_TILE)
    ys = _experts(blk_e, pstart, xs, weg, weu, wed, rows)
    return _combine(dest_flat, ys, h1, w_t.T, wsg.astype(BF16), wsu.astype(BF16), wsd.astype(BF16),
                    ln2_g[None, :], ln2_b[None, :], alpha, MOVE_TILE)


def kernel(x, w_in, b_gate, attn_sink, rel_bias_table, q_norm_g, k_norm_g, w_branch_a, w_branch_b, w_out, ln1_g, ln1_b, w_router, router_bias, w_exp_gate, w_exp_up, w_exp_down, w_sh_gate, w_sh_up, w_sh_down, ln2_g, ln2_b):
    bsz, s, d = x.shape
    depth = w_in.shape[0]
    alpha = (2 * depth) ** 0.25
    outs = []
    for b in range(bsz):
        h = x[b]
        for l in range(depth):
            h = _layer(h, w_in[l], b_gate[l], attn_sink[l], rel_bias_table, q_norm_g[l], k_norm_g[l],
                       w_branch_a[l], w_branch_b[l], w_out[l], ln1_g[l], ln1_b[l], w_router[l], router_bias[l],
                       w_exp_gate[l], w_exp_up[l], w_exp_down[l], w_sh_gate[l], w_sh_up[l], w_sh_down[l],
                       ln2_g[l], ln2_b[l], alpha)
        outs.append(h)
    return jnp.stack(outs, axis=0)
```

```python
import functools
import math

import jax
import jax.numpy as jnp
from jax import lax
from jax.experimental import pallas as pl
from jax.experimental.pallas import tpu as pltpu

F32 = jnp.float32
BF16 = jnp.bfloat16
I32 = jnp.int32

HEAD_DIM = 128
A_HEADS = 8
A_KV_HEADS = 2
B_HEADS = 8
B_KV_HEADS = 2
WINDOW = 128
BLOCK = 128
N_BUCKETS = 32
MAX_DISTANCE = 128
GRID_W = 64
ROPE_THETA = 10000.0
N_GROUPS = 8
TOPK_GROUPS = 4
TOP_K = 8
ROUTED_SCALE = 2.5
LN_EPS = 1e-5
RMS_EPS = 1e-6
NEG_BIG = -1e30

V7X_VMEM_LIMIT = 56 * 1024 * 1024
EXPERT_ROWS = 256
ROUTER_TILE = 256
COMBINE_TILE = 256
DENSE_FAST_BOUND = 50.0


def _params(n_axes):
    return pltpu.CompilerParams(dimension_semantics=("arbitrary",) * n_axes,
                                vmem_limit_bytes=V7X_VMEM_LIMIT)


def _mm_kernel(x_ref, w_ref, o_ref):
    o_ref[...] = jnp.dot(x_ref[...], w_ref[...], preferred_element_type=F32).astype(o_ref.dtype)


def _matmul(x, w, tm, tn, out_dtype, name):
    m, k = x.shape
    n = w.shape[1]
    return pl.pallas_call(
        _mm_kernel,
        grid=(m // tm, n // tn),
        in_specs=[pl.BlockSpec((tm, k), lambda i, j: (i, 0)),
                  pl.BlockSpec((k, tn), lambda i, j: (0, j))],
        out_specs=pl.BlockSpec((tm, tn), lambda i, j: (i, j)),
        out_shape=jax.ShapeDtypeStruct((m, n), out_dtype),
        compiler_params=_params(2),
        name=name,
    )(x, w)


def _norm_rope(x, g, cos, sa, sb):
    x = x.astype(F32)
    ms = jnp.mean(x * x, axis=-1, keepdims=True)
    y = x * lax.rsqrt(ms + RMS_EPS) * g
    out = y * cos + pltpu.roll(y, 96, 1) * sa + pltpu.roll(y, 32, 1) * sb
    return out, jnp.sum(y * y, axis=-1, keepdims=True)


def _lane0(val):
    lane = lax.broadcasted_iota(I32, val.shape, 1)
    return jnp.where(lane == 0, val, 0.0)


def _qkvprep_kernel(khat_ref, x_ref, qg_ref, kg_ref, cos_ref, sa_ref, sb_ref, qo_ref, ko_ref, vo_ref):
    cos, sa, sb = cos_ref[...], sa_ref[...], sb_ref[...]
    ext = 2 * HEAD_DIM
    k0 = B_HEADS * HEAD_DIM
    v0 = k0 + B_KV_HEADS * HEAD_DIM
    ones0 = _lane0(jnp.ones(cos.shape, F32))
    for h in range(B_HEADS):
        q, ss = _norm_rope(x_ref[:, h * HEAD_DIM:(h + 1) * HEAD_DIM], qg_ref[...], cos, sa, sb)
        m = jnp.broadcast_to(jnp.sqrt(ss) * khat_ref[0], q.shape)
        qo_ref[:, h * ext:h * ext + HEAD_DIM] = q.astype(qo_ref.dtype)
        qo_ref[:, h * ext + HEAD_DIM:(h + 1) * ext] = _lane0(-m).astype(qo_ref.dtype)
    for h in range(B_KV_HEADS):
        k, _ = _norm_rope(x_ref[:, k0 + h * HEAD_DIM:k0 + (h + 1) * HEAD_DIM], kg_ref[...], cos, sa, sb)
        ko_ref[:, h * ext:h * ext + HEAD_DIM] = k.astype(ko_ref.dtype)
        ko_ref[:, h * ext + HEAD_DIM:(h + 1) * ext] = ones0.astype(ko_ref.dtype)
        vo_ref[:, h * ext:h * ext + HEAD_DIM] = x_ref[:, v0 + h * HEAD_DIM:v0 + (h + 1) * HEAD_DIM]
        vo_ref[:, h * ext + HEAD_DIM:(h + 1) * ext] = ones0.astype(vo_ref.dtype)


def _qkvprep(qkv, khat, q_gain, k_gain, cos_t, sa_t, sb_t, col0, tr):
    s = qkv.shape[0]
    width = (B_HEADS + 2 * B_KV_HEADS) * HEAD_DIM
    assert col0 % width == 0
    tab = pl.BlockSpec((tr, HEAD_DIM), lambda i: (i, 0))
    gain = pl.BlockSpec((1, HEAD_DIM), lambda i: (0, 0))
    ext = 2 * HEAD_DIM
    return pl.pallas_call(
        _qkvprep_kernel,
        grid=(s // tr,),
        in_specs=[pl.BlockSpec(memory_space=pltpu.SMEM),
                  pl.BlockSpec((tr, width), lambda i: (i, col0 // width)),
                  gain, gain, tab, tab, tab],
        out_specs=[pl.BlockSpec((tr, B_HEADS * ext), lambda i: (i, 0)),
                   pl.BlockSpec((tr, B_KV_HEADS * ext), lambda i: (i, 0)),
                   pl.BlockSpec((tr, B_KV_HEADS * ext), lambda i: (i, 0))],
        out_shape=[jax.ShapeDtypeStruct((s, B_HEADS * ext), BF16),
                   jax.ShapeDtypeStruct((s, B_KV_HEADS * ext), BF16),
                   jax.ShapeDtypeStruct((s, B_KV_HEADS * ext), BF16)],
        compiler_params=_params(1),
        name="qkv_prep",
    )(khat, qkv, q_gain, k_gain, cos_t, sa_t, sb_t)


def _window_kernel(tab_ref, sink_ref, bucket_ref, q_ref, kp_ref, kc_ref, kn_ref, vp_ref, vc_ref, vn_ref,
                   o_ref, bias_scr):
    kvh = pl.program_id(0)
    n = pl.program_id(1)
    nb = pl.num_programs(1)
    g_per = A_HEADS // A_KV_HEADS

    log2e = math.log2(math.e)

    @pl.when((kvh == 0) & (n == 0))
    def _():
        bucket = bucket_ref[...]
        col = lax.broadcasted_iota(I32, bucket.shape, 1)
        for h in range(A_HEADS):
            acc = jnp.full(bucket.shape, NEG_BIG, F32)
            for b in range(N_BUCKETS):
                acc = jnp.where(bucket == b, tab_ref[b * A_HEADS + h] * log2e, acc)
            bias_scr[0, h] = acc
            bias_scr[1, h] = jnp.where(col < BLOCK, NEG_BIG, acc)
            bias_scr[2, h] = jnp.where(col >= 2 * BLOCK, NEG_BIG, acc)

    q = q_ref[...]
    q4 = jnp.concatenate([q[:, g * HEAD_DIM:(g + 1) * HEAD_DIM] for g in range(g_per)], axis=0)
    kcat = jnp.concatenate([kp_ref[...], kc_ref[...], kn_ref[...]], axis=0)
    vcat = jnp.concatenate([vp_ref[...], vc_ref[...], vn_ref[...]], axis=0)
    lane = lax.broadcasted_iota(I32, vcat.shape, 1)
    vext = jnp.concatenate([vcat, jnp.where(lane == 0, 1.0, 0.0).astype(vcat.dtype)], axis=1)
    case = jnp.where(n == 0, 1, jnp.where(n == nb - 1, 2, 0))
    bias4 = bias_scr[case, pl.ds(kvh * g_per, g_per)]
    s = lax.dot_general(q4, kcat, (((1,), (1,)), ((), ())), preferred_element_type=F32)
    s = s + bias4.reshape(g_per * BLOCK, 3 * BLOCK)
    sk = jnp.concatenate([jnp.full((BLOCK, 1), sink_ref[kvh * g_per + g] * log2e, F32) for g in range(g_per)],
                         axis=0)
    m = jnp.maximum(jnp.max(s, axis=-1, keepdims=True), sk)
    p = jnp.exp2(s - m)
    oe = jnp.dot(p.astype(BF16), vext, preferred_element_type=F32)
    o = oe[:, :HEAD_DIM] / (oe[:, HEAD_DIM:HEAD_DIM + 1] + jnp.exp2(sk - m))
    o_ref[...] = jnp.concatenate([o[g * BLOCK:(g + 1) * BLOCK] for g in range(g_per)], axis=1).astype(o_ref.dtype)


def _window_attention(qkv, tab_flat, sink, bucket, q_col0, k_col0, v_col0):
    s = qkv.shape[0]
    nb = s // BLOCK
    assert nb >= 2, "the boundary-case bias tables assume distinct first and last blocks"
    g_per = A_HEADS // A_KV_HEADS
    qw = g_per * HEAD_DIM
    qc, kc, vc = q_col0 // qw, k_col0 // HEAD_DIM, v_col0 // HEAD_DIM

    def kv_spec(c0, shift):
        return pl.BlockSpec((BLOCK, HEAD_DIM),
                            lambda h, n: (jnp.clip(n + shift, 0, nb - 1), c0 + h))

    smem = pl.BlockSpec(memory_space=pltpu.SMEM)
    return pl.pallas_call(
        _window_kernel,
        grid=(A_KV_HEADS, nb),
        in_specs=[smem, smem,
                  pl.BlockSpec((BLOCK, 3 * BLOCK), lambda h, n: (0, 0)),
                  pl.BlockSpec((BLOCK, qw), lambda h, n: (n, qc + h)),
                  kv_spec(kc, -1), kv_spec(kc, 0), kv_spec(kc, 1),
                  kv_spec(vc, -1), kv_spec(vc, 0), kv_spec(vc, 1)],
        out_specs=pl.BlockSpec((BLOCK, qw), lambda h, n: (n, h)),
        out_shape=jax.ShapeDtypeStruct((s, A_HEADS * HEAD_DIM), BF16),
        scratch_shapes=[pltpu.VMEM((3, A_HEADS, BLOCK, 3 * BLOCK), F32)],
        compiler_params=_params(2),
        name="window_attn",
    )(tab_flat, sink, bucket, qkv, qkv, qkv, qkv, qkv, qkv, qkv)


def _stack_heads(q_ref, width):
    g_per = B_HEADS // B_KV_HEADS
    q = q_ref[...]
    ext = 2 * HEAD_DIM
    return jnp.concatenate([q[:, g * ext:g * ext + width] for g in range(g_per)], axis=0)


def _unstack_heads(o, tq):
    g_per = B_HEADS // B_KV_HEADS
    return jnp.concatenate([o[g * tq:(g + 1) * tq] for g in range(g_per)], axis=1)


def _dense_fast_kernel(q_ref, k_ref, v_ref, o_ref, acc_scr, *, tk, unroll):
    tq = q_ref.shape[0]
    s_len = k_ref.shape[0]
    q4 = _stack_heads(q_ref, 2 * HEAD_DIM)
    acc_scr[...] = jnp.zeros(acc_scr.shape, F32)

    def body(c, carry):
        off = pl.multiple_of(c * tk, tk)
        k = k_ref[pl.ds(off, tk), :]
        v = v_ref[pl.ds(off, tk), :]
        s2 = lax.dot_general(q4, k, (((1,), (1,)), ((), ())), preferred_element_type=F32)
        p = jnp.exp2(s2).astype(BF16)
        acc_scr[...] += jnp.dot(p, v, preferred_element_type=F32)
        return carry

    lax.fori_loop(0, s_len // tk, body, 0, unroll=unroll)
    acc = acc_scr[...]
    o = acc[:, :HEAD_DIM] / acc[:, HEAD_DIM:HEAD_DIM + 1]
    o_ref[...] = _unstack_heads(o, tq).astype(o_ref.dtype)


def _dense_exact_kernel(q_ref, k_ref, v_ref, o_ref, m_scr, l_scr, acc_scr, *, tk):
    tq = q_ref.shape[0]
    s_len = k_ref.shape[0]
    q4 = _stack_heads(q_ref, HEAD_DIM)
    m_scr[...] = jnp.full(m_scr.shape, -jnp.inf, F32)
    l_scr[...] = jnp.zeros(l_scr.shape, F32)
    acc_scr[...] = jnp.zeros(acc_scr.shape, F32)

    def body(c, carry):
        off = pl.multiple_of(c * tk, tk)
        k = k_ref[pl.ds(off, tk), :HEAD_DIM]
        v = v_ref[pl.ds(off, tk), :HEAD_DIM]
        s2 = lax.dot_general(q4, k, (((1,), (1,)), ((), ())), preferred_element_type=F32)
        m_prev = m_scr[...]
        m_new = jnp.maximum(m_prev, jnp.max(s2, axis=-1, keepdims=True))
        a = jnp.exp2(m_prev - m_new)
        p = jnp.exp2(s2 - m_new)
        l_scr[...] = a * l_scr[...] + jnp.sum(p, axis=-1, keepdims=True)
        acc_scr[...] = a * acc_scr[...] + jnp.dot(p.astype(BF16), v, preferred_element_type=F32)
        m_scr[...] = m_new
        return carry

    lax.fori_loop(0, s_len // tk, body, 0)
    o = acc_scr[...] / l_scr[...]
    o_ref[...] = _unstack_heads(o, tq).astype(o_ref.dtype)


def _dense_call(kernel_fn, scratch, name, q_ext, k_ext, v_ext, tq):
    s = q_ext.shape[0]
    g_per = B_HEADS // B_KV_HEADS
    ext = 2 * HEAD_DIM
    return pl.pallas_call(
        kernel_fn,
        grid=(B_KV_HEADS, s // tq),
        in_specs=[pl.BlockSpec((tq, g_per * ext), lambda h, i: (i, h)),
                  pl.BlockSpec((s, ext), lambda h, i: (0, h)),
                  pl.BlockSpec((s, ext), lambda h, i: (0, h))],
        out_specs=pl.BlockSpec((tq, g_per * HEAD_DIM), lambda h, i: (i, h)),
        out_shape=jax.ShapeDtypeStruct((s, B_HEADS * HEAD_DIM), BF16),
        scratch_shapes=scratch,
        compiler_params=_params(2),
        name=name,
    )(q_ext, k_ext, v_ext)


def _dense_attention(q_ext, k_ext, v_ext, bound):
    g_per = B_HEADS // B_KV_HEADS

    def fast(q, k, v):
        tq = 256
        return _dense_call(functools.partial(_dense_fast_kernel, tk=512, unroll=4),
                           [pltpu.VMEM((g_per * tq, 2 * HEAD_DIM), F32)], "dense_attn_fast", q, k, v, tq)

    def exact(q, k, v):
        tq = 256
        return _dense_call(functools.partial(_dense_exact_kernel, tk=512),
                           [pltpu.VMEM((g_per * tq, 1), F32), pltpu.VMEM((g_per * tq, 1), F32),
                            pltpu.VMEM((g_per * tq, HEAD_DIM), F32)], "dense_attn_exact", q, k, v, tq)

    return lax.cond(bound <= DENSE_FAST_BOUND, fast, exact, q_ext, k_ext, v_ext)


def _layer_norm(y, g, b):
    mu = jnp.mean(y, axis=-1, keepdims=True)
    yc = y - mu
    var = jnp.mean(yc * yc, axis=-1, keepdims=True)
    return yc * lax.rsqrt(var + LN_EPS) * g + b


def _pack_rows(y):
    c = y.shape[1] // 2
    lo = pltpu.bitcast(y[:, :c].astype(BF16).astype(F32), jnp.uint32)
    hi = pltpu.bitcast(y[:, c:].astype(BF16).astype(F32), jnp.uint32)
    return (lo >> 16) | (hi & jnp.uint32(0xFFFF0000))


def _unpack_rows(p):
    lo = pltpu.bitcast(p << 16, F32)
    hi = pltpu.bitcast(p & jnp.uint32(0xFFFF0000), F32)
    return lo, hi


ROW_CHUNKS = 8


def _store_row_tiles(ref2d, packed):
    r, c = packed.shape
    assert c == ROW_CHUNKS * 128
    for s in range(ROW_CHUNKS):
        ref2d[pl.ds(s, r, stride=ROW_CHUNKS), :] = packed[:, s * 128:(s + 1) * 128]


def _load_row_tiles(ref2d):
    r = ref2d.shape[0] // ROW_CHUNKS
    return jnp.concatenate([ref2d[pl.ds(s, r, stride=ROW_CHUNKS), :] for s in range(ROW_CHUNKS)], axis=1)


def _merge_kernel(oa_ref, ob_ref, ga_ref, gb_ref, bg_ref, x_ref, wa_ref, wb_ref, wo_ref, lg_ref, lb_ref,
                  h_ref, hp_ref, *, alpha):
    ma = jnp.dot(oa_ref[...], wa_ref[...], preferred_element_type=F32)
    mb = jnp.dot(ob_ref[...], wb_ref[...], preferred_element_type=F32)
    gate_a = jax.nn.sigmoid(ga_ref[...].astype(F32) + bg_ref[0:1, :])
    gate_b = jax.nn.sigmoid(gb_ref[...].astype(F32) + bg_ref[1:2, :])
    merged = gate_a * ma + gate_b * mb
    mix = jnp.dot(merged.astype(BF16), wo_ref[...], preferred_element_type=F32)
    h = _layer_norm(alpha * x_ref[...] + mix, lg_ref[...], lb_ref[...])
    h_ref[...] = h
    _store_row_tiles(hp_ref, _pack_rows(h))


def _merge(oa, ob, gates, b_gate, x2, wa, wb, wo, ln_g, ln_b, alpha, tm):
    s, d = x2.shape
    ca = oa.shape[1]

    def whole(shape):
        return pl.BlockSpec(shape, lambda i: (0,) * len(shape), pipeline_mode=pl.Buffered(1))

    return pl.pallas_call(
        functools.partial(_merge_kernel, alpha=alpha),
        grid=(s // tm,),
        in_specs=[pl.BlockSpec((tm, ca), lambda i: (i, 0)),
                  pl.BlockSpec((tm, ca), lambda i: (i, 0)),
                  pl.BlockSpec((tm, d), lambda i: (i, 0)),
                  pl.BlockSpec((tm, d), lambda i: (i, 1)),
                  whole((2, d)),
                  pl.BlockSpec((tm, d), lambda i: (i, 0)),
                  whole((ca, d)), whole((ca, d)), whole((d, d)),
                  whole((1, d)), whole((1, d))],
        out_specs=[pl.BlockSpec((tm, d), lambda i: (i, 0)),
                   pl.BlockSpec((tm * ROW_CHUNKS, 128), lambda i: (i, 0))],
        out_shape=[jax.ShapeDtypeStruct((s, d), F32),
                   jax.ShapeDtypeStruct((s * ROW_CHUNKS, 128), jnp.uint32)],
        compiler_params=_params(1),
        name="merge_ln1",
    )(oa, ob, gates, gates, b_gate, x2, wa, wb, wo, ln_g, ln_b)


def _split_bf16(a):
    hi = a.astype(BF16)
    lo = (a - hi.astype(F32)).astype(BF16)
    return hi, lo


def _router_kernel(h_ref, wr_ref, rb_ref, e_ref, w_ref, r_ref, cnt_ref, carry_scr):
    i = pl.program_id(0)
    tt = h_ref.shape[0]
    n_e = wr_ref.shape[0]
    per = n_e // N_GROUPS

    @pl.when(i == 0)
    def _():
        carry_scr[...] = jnp.zeros(carry_scr.shape, F32)

    h_hi, h_lo = _split_bf16(h_ref[...])
    w_hi, w_lo = _split_bf16(wr_ref[...])
    dn = (((1,), (1,)), ((), ()))
    logits = (lax.dot_general(w_hi, h_hi, dn, preferred_element_type=F32)
              + lax.dot_general(w_hi, h_lo, dn, preferred_element_type=F32)
              + lax.dot_general(w_lo, h_hi, dn, preferred_element_type=F32))
    scores = jax.nn.sigmoid(logits)
    biased = scores + rb_ref[...]

    sc3 = scores.reshape(N_GROUPS, per, tt)
    b3 = biased.reshape(N_GROUPS, per, tt)
    gi = lax.broadcasted_iota(I32, b3.shape, 0)
    ji = lax.broadcasted_iota(I32, b3.shape, 1)
    ei = gi * per + ji

    m1 = jnp.max(b3, axis=1, keepdims=True)
    first = jnp.min(jnp.where(b3 == m1, ji, per), axis=1, keepdims=True)
    m2 = jnp.max(jnp.where(ji == first, -jnp.inf, b3), axis=1, keepdims=True)
    gs = m1 + m2

    g1 = lax.broadcasted_iota(I32, gs.shape, 0)
    gsel = jnp.zeros(gs.shape, F32)
    cur = gs
    for _ in range(TOPK_GROUPS):
        mx = jnp.max(cur, axis=0, keepdims=True)
        fi = jnp.min(jnp.where(cur == mx, g1, N_GROUPS), axis=0, keepdims=True)
        pick = g1 == fi
        gsel = jnp.where(pick, 1.0, gsel)
        cur = jnp.where(pick, -jnp.inf, cur)

    masked = jnp.where(gsel > 0.5, b3, -jnp.inf)
    sel = jnp.zeros(b3.shape, F32)
    picks, e_rows, w_rows = [], [], []
    for _ in range(TOP_K):
        mx = jnp.max(jnp.max(masked, axis=0, keepdims=True), axis=1, keepdims=True)
        fi = jnp.min(jnp.min(jnp.where(masked == mx, ei, n_e), axis=0, keepdims=True), axis=1, keepdims=True)
        pick = ei == fi
        picks.append(pick)
        e_rows.append(fi.reshape(1, tt))
        w_rows.append(jnp.sum(jnp.sum(jnp.where(pick, sc3, 0.0), axis=0, keepdims=True), axis=1,
                              keepdims=True).reshape(1, tt))
        sel = jnp.where(pick, 1.0, sel)
        masked = jnp.where(pick, -jnp.inf, masked)

    wsum = w_rows[0]
    for k in range(1, TOP_K):
        wsum = wsum + w_rows[k]

    sel2 = sel.reshape(n_e, tt)
    rr = lax.broadcasted_iota(I32, (tt, tt), 0)
    cc = lax.broadcasted_iota(I32, (tt, tt), 1)
    upper = jnp.where(rr < cc, 1.0, 0.0).astype(BF16)
    before = jnp.dot(sel2.astype(BF16), upper, preferred_element_type=F32) + carry_scr[...]
    before3 = before.reshape(N_GROUPS, per, tt)
    for k in range(TOP_K):
        rk = jnp.sum(jnp.sum(jnp.where(picks[k], before3, 0.0), axis=0, keepdims=True), axis=1, keepdims=True)
        e_ref[k:k + 1, :] = e_rows[k]
        w_ref[k:k + 1, :] = w_rows[k] / wsum * ROUTED_SCALE
        r_ref[k:k + 1, :] = rk.reshape(1, tt).astype(I32)

    carry_scr[...] = carry_scr[...] + jnp.sum(sel2, axis=1, keepdims=True)
    cnt_ref[...] = carry_scr[...].astype(I32)


def _router(h1, wr_t, rb_col, tt):
    s, d = h1.shape
    n_e = wr_t.shape[0]
    out3 = lambda dt: jax.ShapeDtypeStruct((TOP_K, s), dt)
    return pl.pallas_call(
        _router_kernel,
        grid=(s // tt,),
        in_specs=[pl.BlockSpec((tt, d), lambda i: (i, 0)),
                  pl.BlockSpec((n_e, d), lambda i: (0, 0)),
                  pl.BlockSpec((n_e, 1), lambda i: (0, 0))],
        out_specs=[pl.BlockSpec((TOP_K, tt), lambda i: (0, i)),
                   pl.BlockSpec((TOP_K, tt), lambda i: (0, i)),
                   pl.BlockSpec((TOP_K, tt), lambda i: (0, i)),
                   pl.BlockSpec((n_e, 1), lambda i: (0, 0))],
        out_shape=[out3(I32), out3(F32), out3(I32), jax.ShapeDtypeStruct((n_e, 1), I32)],
        scratch_shapes=[pltpu.VMEM((n_e, 1), F32)],
        compiler_params=_params(1),
        name="router_topk",
    )(h1, wr_t, rb_col)


def _finalize_kernel(cnt_ref, e_ref, r_ref, dest_ref, pstart_ref, blk_ref, *, rows, n_blocks):
    n_e = cnt_ref.shape[0]

    def scan(e, start):
        pstart_ref[e] = start
        nblk = (cnt_ref[e] + rows - 1) // rows
        b0 = start // rows

        def fill(j, c):
            blk_ref[b0 + j] = e
            return c

        lax.fori_loop(0, nblk, fill, 0)
        return start + nblk * rows

    end = lax.fori_loop(0, n_e, scan, 0)
    pstart_ref[n_e] = end

    def tail(b, c):
        blk_ref[b] = n_e - 1
        return c

    lax.fori_loop(end // rows, n_blocks, tail, 0)

    chunk = 1024
    for c in range(e_ref.shape[1] // chunk):
        ev = e_ref[:, c * chunk:(c + 1) * chunk]

        def add(e, acc):
            return jnp.where(ev == e, pstart_ref[e], acc)

        base = lax.fori_loop(0, n_e, add, jnp.zeros(ev.shape, I32))
        dest_ref[:, c * chunk:(c + 1) * chunk] = base + r_ref[:, c * chunk:(c + 1) * chunk]


def _finalize(counts, e_t, r_t, rows, n_blocks):
    n_e = counts.shape[0]
    smem = pl.BlockSpec(memory_space=pltpu.SMEM)
    vmem = pl.BlockSpec(memory_space=pltpu.VMEM)
    return pl.pallas_call(
        functools.partial(_finalize_kernel, rows=rows, n_blocks=n_blocks),
        in_specs=[smem, vmem, vmem],
        out_specs=[vmem, smem, smem],
        out_shape=[jax.ShapeDtypeStruct(e_t.shape, I32),
                   jax.ShapeDtypeStruct((n_e + 1,), I32),
                   jax.ShapeDtypeStruct((n_blocks,), I32)],
        compiler_params=pltpu.CompilerParams(vmem_limit_bytes=V7X_VMEM_LIMIT),
        name="route_finalize",
    )(counts, e_t, r_t)


def _invert_kernel(cnt_ref, pstart_ref, dest_hbm, inv_ref, chunk_a, chunk_b, sem, *, n_tok, chunk):
    n_assign = n_tok * TOP_K
    n_e = cnt_ref.shape[0]
    total_rows = inv_ref.shape[0]
    n_pairs = n_assign // (2 * chunk)
    group = 8 * TOP_K

    def chunk_copy(c, buf, s):
        return pltpu.make_async_copy(dest_hbm.at[pl.ds(c * chunk, chunk)], buf, sem.at[s])

    chunk_copy(0, chunk_a, 0).start()

    def per_expert(e, nxt_id):
        lo = pstart_ref[e] + cnt_ref[e]
        hi = jnp.where(e == n_e - 1, total_rows, pstart_ref[e + 1])

        def fill(p, c):
            inv_ref[p] = nxt_id + (p - lo)
            return c

        lax.fori_loop(lo, hi, fill, 0)
        return nxt_id + (hi - lo)

    lax.fori_loop(0, n_e, per_expert, n_assign)

    def process(buf, c):
        tok0 = c * (chunk // TOP_K)

        def per_group(g, carry):
            base = g * group
            tok = tok0 + g * (group // TOP_K)
            for j in range(group):
                inv_ref[buf[base + j]] = (j % TOP_K) * n_tok + (tok + j // TOP_K)
            return carry

        lax.fori_loop(0, chunk // group, per_group, 0)

    def per_pair(i, carry):
        c0 = 2 * i
        chunk_copy(c0, chunk_a, 0).wait()
        chunk_copy(c0 + 1, chunk_b, 1).start()
        process(chunk_a, c0)
        chunk_copy(c0 + 1, chunk_b, 1).wait()

        @pl.when(i + 1 < n_pairs)
        def _():
            chunk_copy(c0 + 2, chunk_a, 0).start()

        process(chunk_b, c0 + 1)
        return carry

    lax.fori_loop(0, n_pairs, per_pair, 0)


def _invert(counts, pstart, dest_flat, n_tok, total_rows):
    chunk = 2048
    assert (n_tok * TOP_K) % (2 * chunk) == 0
    smem = pl.BlockSpec(memory_space=pltpu.SMEM)
    return pl.pallas_call(
        functools.partial(_invert_kernel, n_tok=n_tok, chunk=chunk),
        in_specs=[smem, smem, pl.BlockSpec(memory_space=pl.ANY)],
        out_specs=smem,
        out_shape=jax.ShapeDtypeStruct((total_rows,), I32),
        scratch_shapes=[pltpu.SMEM((chunk,), I32), pltpu.SMEM((chunk,), I32), pltpu.SemaphoreType.DMA((2,))],
        name="route_invert",
    )(counts, pstart, dest_flat)


def _expert_kernel(blk_ref, pstart_ref, inv_hbm, hp_hbm, wg_hbm, wu_hbm, wd_hbm, comb_hbm,
                   idx_s, xbuf, ybuf, wg_st, wu_st, wd_st, wg_s, wu_s, wd_s, isem, gsem, ssem, wsem, zsem,
                   *, n_tok):
    b = pl.program_id(0)
    rc = ROW_CHUNKS
    rows = xbuf.shape[1] // rc
    n_assign = n_tok * TOP_K
    n_e = pstart_ref.shape[0] - 1
    n_used = pstart_ref[n_e] // rows
    n_trash = (comb_hbm.shape[0] // rc - n_assign) // rows
    groups = 4

    def tile_of(i):
        return pl.ds(i * rc, rc) if isinstance(i, int) else pl.ds(pl.multiple_of(i * rc, rc), rc)

    def idx_fetch(blk):
        return pltpu.make_async_copy(inv_hbm.at[blk], idx_s.at[blk % 4], isem.at[blk % 4])

    def row_id(blk, r):
        return idx_s[blk % 4, r >> 7, r & 127]

    def gather_row(blk, r):
        v = row_id(blk, r)
        tok = (v & (n_tok - 1)) if n_tok & (n_tok - 1) == 0 else lax.rem(v, n_tok)
        return pltpu.make_async_copy(hp_hbm.at[tile_of(tok)], xbuf.at[blk % 2, tile_of(r)], gsem.at[blk % 2])

    def scatter_row(blk, r):
        return pltpu.make_async_copy(ybuf.at[blk % 2, tile_of(r)], comb_hbm.at[tile_of(row_id(blk, r))],
                                     ssem.at[blk % 2])

    def wait_rows(par, scatter):
        for _ in range(rows):
            if scatter:
                pltpu.make_async_copy(ybuf.at[par, tile_of(0)], comb_hbm.at[tile_of(0)], ssem.at[par]).wait()
            else:
                pltpu.make_async_copy(hp_hbm.at[tile_of(0)], xbuf.at[par, tile_of(0)], gsem.at[par]).wait()

    def weight_copies(e):
        return (pltpu.make_async_copy(wg_hbm.at[e], wg_st, wsem.at[0]),
                pltpu.make_async_copy(wu_hbm.at[e], wu_st, wsem.at[1]),
                pltpu.make_async_copy(wd_hbm.at[e], wd_st, wsem.at[2]))

    def loop_rows(fn):
        def body(r8, c):
            for j in range(8):
                fn(r8 * 8 + j)
            return c
        lax.fori_loop(0, rows // 8, body, 0)

    @pl.when(b == 0)
    def _():
        ybuf[1] = jnp.zeros(ybuf.shape[1:], ybuf.dtype)

        def fill(j):
            return pltpu.make_async_copy(ybuf.at[1], comb_hbm.at[pl.ds((n_assign + j * rows) * rc, rows * rc)], zsem)

        for j in range(n_trash):
            fill(j).start()
        for j in range(n_trash):
            fill(j).wait()
        idx_fetch(0).start()
        idx_fetch(1).start()
        idx_fetch(0).wait()
        idx_fetch(1).wait()
        loop_rows(lambda r: gather_row(0, r).start())
        for cp in weight_copies(blk_ref[0]):
            cp.start()

    @pl.when(b < n_used)
    def _():
        slot = b % 2
        e = blk_ref[b]

        @pl.when(b + 2 <= n_used)
        def _():
            idx_fetch(b + 2).start()

        wait_rows(slot, scatter=False)

        @pl.when(b >= 2)
        def _():
            wait_rows(slot, scatter=True)

        @pl.when((b == 0) | (e != blk_ref[jnp.maximum(b - 1, 0)]))
        def _():
            for cp in weight_copies(e):
                cp.wait()
            wg_s[...] = wg_st[...].astype(BF16)
            wu_s[...] = wu_st[...].astype(BF16)
            wd_s[...] = wd_st[...].astype(BF16)
            nxt = b + (pstart_ref[e + 1] - pstart_ref[e]) // rows

            @pl.when(nxt < n_used)
            def _():
                for cp in weight_copies(blk_ref[nxt]):
                    cp.start()

        per = rows // groups

        def issue(grp, with_scatter):
            for r in range(grp * per, (grp + 1) * per):
                gather_row(b + 1, r).start()
                if with_scatter:
                    scatter_row(b - 1, r).start()

        def compute(with_scatter):
            lo, hi = _unpack_rows(_load_row_tiles(xbuf.at[slot]))
            x = jnp.concatenate([lo, hi], axis=1).astype(BF16)
            issue(0, with_scatter)
            g = jnp.dot(x, wg_s[...], preferred_element_type=F32)
            issue(1, with_scatter)
            u = jnp.dot(x, wu_s[...], preferred_element_type=F32)
            issue(2, with_scatter)
            hid = (jax.nn.silu(g) * u).astype(BF16)
            y = jnp.dot(hid, wd_s[...], preferred_element_type=F32)
            issue(3, with_scatter)
            _store_row_tiles(ybuf.at[slot], _pack_rows(y))

        @pl.when(b == 0)
        def _():
            compute(False)

        @pl.when(b > 0)
        def _():
            compute(True)

        @pl.when(b + 2 <= n_used)
        def _():
            idx_fetch(b + 2).wait()

    @pl.when(b == n_used)
    def _():
        wait_rows(b % 2, scatter=False)

        @pl.when(b >= 2)
        def _():
            wait_rows(b % 2, scatter=True)

        loop_rows(lambda r: scatter_row(b - 1, r).start())
        wait_rows((b - 1) % 2, scatter=True)


def _experts(blk_e, pstart, inv3, hp, wg, wu, wd, n_tok, rows):
    n_blocks = inv3.shape[0]
    d, f = wg.shape[1], wg.shape[2]
    assert d == 2 * ROW_CHUNKS * 128 and hp.shape[1] == 128
    anyspec = pl.BlockSpec(memory_space=pl.ANY)
    buf = pltpu.VMEM((2, rows * ROW_CHUNKS, 128), jnp.uint32)
    grid_spec = pltpu.PrefetchScalarGridSpec(
        num_scalar_prefetch=2,
        grid=(n_blocks,),
        in_specs=[anyspec] * 5,
        out_specs=anyspec,
        scratch_shapes=[pltpu.SMEM((4, rows // 128, 128), I32), buf, buf,
                        pltpu.VMEM((d, f), F32), pltpu.VMEM((d, f), F32), pltpu.VMEM((f, d), F32),
                        pltpu.VMEM((d, f), BF16), pltpu.VMEM((d, f), BF16), pltpu.VMEM((f, d), BF16),
                        pltpu.SemaphoreType.DMA((4,)), pltpu.SemaphoreType.DMA((2,)),
                        pltpu.SemaphoreType.DMA((2,)), pltpu.SemaphoreType.DMA((3,)), pltpu.SemaphoreType.DMA],
    )
    return pl.pallas_call(
        functools.partial(_expert_kernel, n_tok=n_tok),
        grid_spec=grid_spec,
        out_shape=jax.ShapeDtypeStruct((n_blocks * rows * ROW_CHUNKS, 128), jnp.uint32),
        compiler_params=_params(1),
        name="moe_experts",
    )(blk_e, pstart, inv3, hp, wg, wu, wd)


def _combine_kernel(*refs, alpha):
    comb_refs = refs[:TOP_K]
    h_ref, w_ref, sg_ref, su_ref, sd_ref, lg_ref, lb_ref, o_ref = refs[TOP_K:]
    h = h_ref[...]
    hb = h.astype(BF16)
    g = jnp.dot(hb, sg_ref[...], preferred_element_type=F32)
    u = jnp.dot(hb, su_ref[...], preferred_element_type=F32)
    hid = (jax.nn.silu(g) * u).astype(BF16)
    shared = jnp.dot(hid, sd_ref[...], preferred_element_type=F32)

    w = w_ref[...]
    r_lo, r_hi = None, None
    for k in range(TOP_K):
        lo, hi = _unpack_rows(_load_row_tiles(comb_refs[k]))
        wk = w[:, k:k + 1]
        r_lo = wk * lo if r_lo is None else r_lo + wk * lo
        r_hi = wk * hi if r_hi is None else r_hi + wk * hi
    routed = jnp.concatenate([r_lo, r_hi], axis=1)
    o_ref[...] = _layer_norm(alpha * h + (routed + shared), lg_ref[...], lb_ref[...])


def _combine(comb, h1, w_tok, sg, su, sd, ln_g, ln_b, alpha, tile):
    s, d = h1.shape
    f = sg.shape[1]
    per_slot = s // tile

    def whole(shape):
        return pl.BlockSpec(shape, lambda i: (0,) * len(shape), pipeline_mode=pl.Buffered(1))

    def slot_spec(k):
        return pl.BlockSpec((tile * ROW_CHUNKS, 128), lambda i: (k * per_slot + i, 0))

    return pl.pallas_call(
        functools.partial(_combine_kernel, alpha=alpha),
        grid=(s // tile,),
        in_specs=[slot_spec(k) for k in range(TOP_K)]
                 + [pl.BlockSpec((tile, d), lambda i: (i, 0)),
                    pl.BlockSpec((tile, TOP_K), lambda i: (i, 0)),
                    whole((d, f)), whole((d, f)), whole((f, d)),
                    whole((1, d)), whole((1, d))],
        out_specs=pl.BlockSpec((tile, d), lambda i: (i, 0)),
        out_shape=jax.ShapeDtypeStruct((s, d), F32),
        compiler_params=_params(1),
        name="moe_combine_ln2",
    )(*([comb] * TOP_K), h1, w_tok, sg, su, sd, ln_g, ln_b)


def _t5_bucket(rel):
    nb = N_BUCKETS // 2
    ret = jnp.where(rel > 0, nb, 0)
    n = jnp.abs(rel)
    max_exact = nb // 2
    nf = jnp.maximum(n, 1).astype(F32)
    large = max_exact + (jnp.log(nf / max_exact) / math.log(MAX_DISTANCE / max_exact) * (nb - max_exact)).astype(I32)
    large = jnp.minimum(large, nb - 1)
    return ret + jnp.where(n < max_exact, n, large)


def _window_bucket_map():
    qi = jnp.arange(BLOCK)[:, None]
    c = jnp.arange(3 * BLOCK)[None, :]
    rel = c - BLOCK - qi
    return jnp.where(jnp.abs(rel) <= WINDOW, _t5_bucket(rel), -1).astype(I32)


def _rope_tables(s):
    half = HEAD_DIM // 2
    qd = half // 2
    pos = jnp.arange(s)
    row = (pos // GRID_W).astype(F32)
    col = (pos % GRID_W).astype(F32)
    inv = ROPE_THETA ** (-jnp.arange(0, half, 2, dtype=F32) / half)
    ang_r = row[:, None] * inv
    ang_c = col[:, None] * inv
    cr, sr, cc, sc = jnp.cos(ang_r), jnp.sin(ang_r), jnp.cos(ang_c), jnp.sin(ang_c)
    z = jnp.zeros((s, qd), F32)
    cos_t = jnp.concatenate([cr, cr, cc, cc], axis=1)
    sa_t = jnp.concatenate([-sr, z, -sc, z], axis=1)
    sb_t = jnp.concatenate([z, sr, z, sc], axis=1)
    return cos_t, sa_t, sb_t


def _layer(h, w_in, b_gate, sink, rel_table, qn_g, kn_g, wba, wbb, w_out, ln1_g, ln1_b, w_router, r_bias,
           weg, weu, wed, wsg, wsu, wsd, ln2_g, ln2_b, alpha):
    s, d = h.shape
    a_q, a_kv = A_HEADS * HEAD_DIM, A_KV_HEADS * HEAD_DIM
    b_q, b_kv = B_HEADS * HEAD_DIM, B_KV_HEADS * HEAD_DIM
    n_qkv = a_q + 2 * a_kv + b_q + 2 * b_kv
    n_e = w_router.shape[1]

    xb = h.astype(BF16)
    qa_scale = jnp.where(jnp.arange(w_in.shape[1]) < a_q, HEAD_DIM ** -0.5 * math.log2(math.e), 1.0)
    w_in_b = (w_in * qa_scale.astype(F32)).astype(BF16)
    qkv = _matmul(xb, w_in_b[:, :n_qkv], 1024, 1024, BF16, "in_proj_qkv")
    gates = _matmul(xb, w_in_b[:, n_qkv:], 1024, 1024, BF16, "in_proj_gates")

    oa = _window_attention(qkv, rel_table.reshape(-1), sink, _window_bucket_map(), 0, a_q, a_q + a_kv)

    qb0 = a_q + 2 * a_kv
    q_gain = (qn_g * (HEAD_DIM ** -0.5 * math.log2(math.e)))[None, :]
    khat = (HEAD_DIM ** 0.5) * jnp.max(jnp.abs(kn_g))
    bound = (HEAD_DIM ** 0.5) * jnp.max(jnp.abs(q_gain)) * khat
    cos_t, sa_t, sb_t = _rope_tables(s)
    q_ext, k_ext, v_ext = _qkvprep(qkv, khat.reshape(1), q_gain, kn_g[None, :], cos_t, sa_t, sb_t, qb0, 512)
    ob = _dense_attention(q_ext, k_ext, v_ext, bound)

    h1, h1p = _merge(oa, ob, gates, b_gate, h, wba.astype(BF16), wbb.astype(BF16), w_out.astype(BF16),
                     ln1_g[None, :], ln1_b[None, :], alpha, 256)

    e_t, w_t, r_t, counts = _router(h1, w_router.T, r_bias[:, None], ROUTER_TILE)
    rows = EXPERT_ROWS
    total_rows = s * TOP_K + n_e * rows
    n_blocks = total_rows // rows
    counts = counts.reshape(n_e)
    dest_t, pstart, blk_e = _finalize(counts, e_t, r_t, rows, n_blocks)
    dest_flat = dest_t.T.reshape(-1)
    inv = _invert(counts, pstart, dest_flat, s, total_rows)
    comb = _experts(blk_e, pstart, inv.reshape(n_blocks, rows // 128, 128), h1p, weg, weu, wed, s, rows)
    return _combine(comb, h1, w_t.T, wsg.astype(BF16), wsu.astype(BF16), wsd.astype(BF16),
                    ln2_g[None, :], ln2_b[None, :], alpha, COMBINE_TILE)


def kernel(x, w_in, b_gate, attn_sink, rel_bias_table, q_norm_g, k_norm_g, w_branch_a, w_branch_b, w_out, ln1_g, ln1_b, w_router, router_bias, w_exp_gate, w_exp_up, w_exp_down, w_sh_gate, w_sh_up, w_sh_down, ln2_g, ln2_b):
    bsz, s, d = x.shape
    depth = w_in.shape[0]
    alpha = (2 * depth) ** 0.25
    outs = []
    for b in range(bsz):
        h = x[b]
        for l in range(depth):
            h = _layer(h, w_in[l], b_gate[l], attn_sink[l], rel_bias_table, q_norm_g[l], k_norm_g[l],
                       w_branch_a[l], w_branch_b[l], w_out[l], ln1_g[l], ln1_b[l], w_router[l], router_bias[l],
                       w_exp_gate[l], w_exp_up[l], w_exp_down[l], w_sh_gate[l], w_sh_up[l], w_sh_down[l],
                       ln2_g[l], ln2_b[l], alpha)
        outs.append(h)
    return jnp.stack(outs, axis=0)
```
